```python
import jax, jax.numpy as jnp
from jax import lax
import numpy as np

D_MODEL = 1024
BATCH = 16
SEQ = 4096
DEPTH = 4

N_A_LAYERS = DEPTH // 2
N_B_LAYERS = DEPTH - N_A_LAYERS
POOL_WINDOWS = (2, 4, 8, 16)
N_POOL_GROUPS = len(POOL_WINDOWS)
POOL_GROUP_DIM = D_MODEL // N_POOL_GROUPS
QK_NOPE_DIM = 128
QK_ROPE_DIM = 64
V_HEAD_DIM = 128
N_HEADS = D_MODEL // 128
Q_LORA_RANK = D_MODEL // 2
KV_LORA_RANK = D_MODEL // 4
ROPE_THETA = 10000.0
D_FF = 4 * D_MODEL
Q_BLOCK = 128
N_MOD = 6
DEEPNORM_ALPHA = (2.0 * DEPTH) ** 0.25
DEEPNORM_BETA = (8.0 * DEPTH) ** -0.25
LN_EPS = 1e-5
RMS_EPS = 1e-6
MAX_START = 1024
ATTN_SCALE = (QK_NOPE_DIM + QK_ROPE_DIM) ** -0.5

kernel_name = "yoco_pool_mla_deepnorm_adaln"


def layer_norm(x, g, b):
    xf = x.astype(jnp.float32)
    mu = jnp.mean(xf, axis=-1, keepdims=True)
    xc = xf - mu
    var = jnp.mean(xc * xc, axis=-1, keepdims=True)
    return (xc * lax.rsqrt(var + LN_EPS) * g + b).astype(x.dtype)


def rms_norm(x, g):
    xf = x.astype(jnp.float32)
    ms = jnp.mean(xf * xf, axis=-1, keepdims=True)
    return (xf * lax.rsqrt(ms + RMS_EPS) * g).astype(x.dtype)


def rope(x, cos, sin):
    x1, x2 = jnp.split(x, 2, axis=-1)
    return jnp.concatenate([x1 * cos - x2 * sin, x2 * cos + x1 * sin], axis=-1)


def modulate(x, shift, scale):
    return x * (1.0 + scale[:, None, :]) + shift[:, None, :]


def causal_multiscale_pool(h):
    b, s, d = h.shape
    hf = h.astype(jnp.float32)
    cs = jnp.cumsum(hf, axis=1)
    t = jnp.arange(s)
    outs = []
    for g, w in enumerate(POOL_WINDOWS):
        csg = cs[..., g * POOL_GROUP_DIM:(g + 1) * POOL_GROUP_DIM]
        shifted = jnp.pad(csg[:, :s - w], ((0, 0), (w, 0), (0, 0)))
        cnt = jnp.minimum(t + 1, w).astype(jnp.float32)[None, :, None]
        outs.append((csg - shifted) / cnt)
    pooled = jnp.concatenate(outs, axis=-1)
    return (pooled - hf).astype(h.dtype)


def pool_mixer(h, w_pool, scale):
    b, s, d = h.shape
    y = causal_multiscale_pool(h).reshape(b, s, N_POOL_GROUPS, POOL_GROUP_DIM)
    y = jnp.einsum('bsgc,gcd->bsgd', y, w_pool).reshape(b, s, d)
    return y * scale


def sq_relu_mlp(h, w1, w2):
    a = jax.nn.relu(h @ w1)
    return (a * a) @ w2


def shared_kv(x, kv_in_w, kv_norm_g, k_up_w, v_up_w, cos, sin):
    b, s, _ = x.shape
    ckr = x @ kv_in_w
    c_kv = rms_norm(ckr[..., :KV_LORA_RANK], kv_norm_g)
    k_rope = rope(ckr[..., KV_LORA_RANK:], cos, sin)
    k_nope = (c_kv @ k_up_w).reshape(b, s, N_HEADS, QK_NOPE_DIM)
    v = (c_kv @ v_up_w).reshape(b, s, N_HEADS, V_HEAD_DIM)
    return k_nope, k_rope, v


def mla_attention(h, q_down_w, q_norm_g, q_up_w, out_w, k_nope, k_rope, v, cos, sin):
    b, s, _ = h.shape
    cq = rms_norm(h @ q_down_w, q_norm_g)
    q = (cq @ q_up_w).reshape(b, s, N_HEADS, QK_NOPE_DIM + QK_ROPE_DIM)
    q_nope = q[..., :QK_NOPE_DIM]
    q_rope = rope(q[..., QK_NOPE_DIM:], cos[:, :, None, :], sin[:, :, None, :])
    nb = s // Q_BLOCK
    qn = q_nope.reshape(b, nb, Q_BLOCK, N_HEADS, QK_NOPE_DIM).transpose(1, 0, 2, 3, 4)
    qr = q_rope.reshape(b, nb, Q_BLOCK, N_HEADS, QK_ROPE_DIM).transpose(1, 0, 2, 3, 4)
    kpos = jnp.arange(s)

    def one_block(args):
        qn_b, qr_b, i = args
        sc = (jnp.einsum('bqhd,bkhd->bhqk', qn_b, k_nope)
              + jnp.einsum('bqhr,bkr->bhqk', qr_b, k_rope)).astype(jnp.float32) * ATTN_SCALE
        qpos = i * Q_BLOCK + jnp.arange(Q_BLOCK)
        mask = kpos[None, :] <= qpos[:, None]
        sc = jnp.where(mask[None, None], sc, -jnp.inf)
        p = jax.nn.softmax(sc, axis=-1).astype(v.dtype)
        return jnp.einsum('bhqk,bkhd->bqhd', p, v)

    o = lax.map(one_block, (qn, qr, jnp.arange(nb)))
    o = o.transpose(1, 0, 2, 3, 4).reshape(b, s, N_HEADS * V_HEAD_DIM)
    return o @ out_w


def setup_inputs(seed: int = 0) -> dict:
    key = jax.random.key(seed)
    ks = jax.random.split(key, 20)
    n = jax.random.normal
    f32 = jnp.float32
    D = D_MODEL
    qk_dim = QK_NOPE_DIM + QK_ROPE_DIM
    start = jax.random.randint(ks[2], (BATCH, 1), 0, MAX_START, dtype=jnp.int32)
    positions = (start + jnp.arange(SEQ, dtype=jnp.int32)[None, :]).astype(jnp.int32)
    return {
        "x": n(ks[0], (BATCH, SEQ, D), f32),
        "c": n(ks[1], (BATCH, D), f32),
        "positions": positions,
        "ada_w": n(ks[3], (DEPTH, D, N_MOD * D), f32) * (0.5 * D ** -0.5),
        "ada_b": 0.01 * n(ks[4], (DEPTH, N_MOD * D), f32),
        "ln_g": 1.0 + 0.02 * n(ks[5], (DEPTH, 2, D), f32),
        "ln_b": 0.02 * n(ks[6], (DEPTH, 2, D), f32),
        "mlp_w1": n(ks[7], (DEPTH, D, D_FF), f32) * D ** -0.5,
        "mlp_w2": n(ks[8], (DEPTH, D_FF, D), f32) * (D_FF ** -0.5 * DEEPNORM_BETA),
        "pool_w": n(ks[9], (N_A_LAYERS, N_POOL_GROUPS, POOL_GROUP_DIM, POOL_GROUP_DIM), f32)
                  * (POOL_GROUP_DIM ** -0.5 * DEEPNORM_BETA),
        "pool_scale": 1.0 + 0.1 * n(ks[10], (N_A_LAYERS, D), f32),
        "q_down_w": n(ks[11], (N_B_LAYERS, D, Q_LORA_RANK), f32) * D ** -0.5,
        "q_norm_g": 1.0 + 0.02 * n(ks[12], (N_B_LAYERS, Q_LORA_RANK), f32),
        "q_up_w": n(ks[13], (N_B_LAYERS, Q_LORA_RANK, N_HEADS * qk_dim), f32) * Q_LORA_RANK ** -0.5,
        "attn_out_w": n(ks[14], (N_B_LAYERS, N_HEADS * V_HEAD_DIM, D), f32)
                      * ((N_HEADS * V_HEAD_DIM) ** -0.5 * DEEPNORM_BETA),
        "kv_in_w": n(ks[15], (D, KV_LORA_RANK + QK_ROPE_DIM), f32) * D ** -0.5,
        "kv_norm_g": 1.0 + 0.02 * n(ks[16], (KV_LORA_RANK,), f32),
        "k_up_w": n(ks[17], (KV_LORA_RANK, N_HEADS * QK_NOPE_DIM), f32) * KV_LORA_RANK ** -0.5,
        "v_up_w": n(ks[18], (KV_LORA_RANK, N_HEADS * V_HEAD_DIM), f32)
                  * (KV_LORA_RANK ** -0.5 * DEEPNORM_BETA),
    }


def reference(x, c, positions, ada_w, ada_b, ln_g, ln_b, mlp_w1, mlp_w2, pool_w, pool_scale,
              q_down_w, q_norm_g, q_up_w, attn_out_w, kv_in_w, kv_norm_g, k_up_w, v_up_w):
    b, s, d = x.shape
    mod = (jnp.einsum('bd,ldm->blm', jax.nn.silu(c), ada_w) + ada_b[None]).reshape(b, DEPTH, N_MOD, d)
    inv_freq = ROPE_THETA ** (-jnp.arange(0, QK_ROPE_DIM, 2, dtype=jnp.float32) / QK_ROPE_DIM)
    ang = positions.astype(jnp.float32)[..., None] * inv_freq
    cos = jnp.cos(ang).astype(x.dtype)
    sin = jnp.sin(ang).astype(x.dtype)

    for l in range(DEPTH):
        shift1, scale1, gate1 = mod[:, l, 0], mod[:, l, 1], mod[:, l, 2]
        shift2, scale2, gate2 = mod[:, l, 3], mod[:, l, 4], mod[:, l, 5]
        h = modulate(x, shift1, scale1)
        if l < N_A_LAYERS:
            y = pool_mixer(h, pool_w[l], pool_scale[l])
        else:
            if l == N_A_LAYERS:
                k_nope, k_rope, v = shared_kv(x, kv_in_w, kv_norm_g, k_up_w, v_up_w, cos, sin)
            j = l - N_A_LAYERS
            y = mla_attention(h, q_down_w[j], q_norm_g[j], q_up_w[j], attn_out_w[j],
                              k_nope, k_rope, v, cos, sin)
        x = layer_norm(DEEPNORM_ALPHA * x + gate1[:, None, :] * y, ln_g[l, 0], ln_b[l, 0])
        h = modulate(x, shift2, scale2)
        y = sq_relu_mlp(h, mlp_w1[l], mlp_w2[l])
        x = layer_norm(DEEPNORM_ALPHA * x + gate2[:, None, :] * y, ln_g[l, 1], ln_b[l, 1])
    return x
```

```python
import functools

import numpy as np
import jax
import jax.numpy as jnp
from jax import lax
from jax.experimental import pallas as pl
from jax.experimental.pallas import tpu as pltpu

D_MODEL = 1024
DEPTH = 4
N_A_LAYERS = DEPTH // 2
POOL_WINDOWS = (2, 4, 8, 16)
N_POOL_GROUPS = len(POOL_WINDOWS)
POOL_GROUP_DIM = D_MODEL // N_POOL_GROUPS
POOL_HALO = 16
QK_NOPE_DIM = 128
QK_ROPE_DIM = 64
V_HEAD_DIM = 128
N_HEADS = D_MODEL // 128
Q_LORA_RANK = D_MODEL // 2
KV_LORA_RANK = D_MODEL // 4
ROPE_THETA = 10000.0
D_FF = 4 * D_MODEL
N_MOD = 6
DEEPNORM_ALPHA = (2.0 * DEPTH) ** 0.25
LN_EPS = 1e-5
RMS_EPS = 1e-6
ATTN_SCALE = (QK_NOPE_DIM + QK_ROPE_DIM) ** -0.5

LANES = 128
HEAD_W = 2 * LANES
ROPE_HALF = QK_ROPE_DIM // 2

VMEM_LIMIT = 56 * 1024 * 1024

F32 = jnp.float32
BF16 = jnp.bfloat16


def _params(semantics):
    return pltpu.CompilerParams(dimension_semantics=semantics, vmem_limit_bytes=VMEM_LIMIT)


def _const_spec(shape):
    nd = len(shape)
    return pl.BlockSpec(shape, lambda *_: (0,) * nd, pipeline_mode=pl.Buffered(1))


def _layer_norm(z, g, b):
    mu = jnp.mean(z, axis=-1, keepdims=True)
    zc = z - mu
    var = jnp.mean(zc * zc, axis=-1, keepdims=True)
    return zc * lax.rsqrt(var + LN_EPS) * g + b


def _rope_pair_sum(y):
    return y + pltpu.roll(y, QK_ROPE_DIM, axis=1)


def _adaln_kernel(c_ref, w_ref, b_ref, o_ref):
    c = c_ref[...]
    sc = c * jax.nn.sigmoid(c)
    o_ref[0] = jnp.dot(sc.astype(BF16), w_ref[0].astype(BF16),
                       preferred_element_type=F32) + b_ref[0]


def _adaln_mod(c, ada_w, ada_b):
    b, d = c.shape
    n = ada_w.shape[-1]
    tn = 1536
    return pl.pallas_call(
        _adaln_kernel,
        out_shape=jax.ShapeDtypeStruct((DEPTH, b, n), F32),
        grid=(DEPTH, n // tn),
        in_specs=[
            pl.BlockSpec((b, d), lambda l, j: (0, 0)),
            pl.BlockSpec((1, d, tn), lambda l, j: (l, 0, j)),
            pl.BlockSpec((1, 1, tn), lambda l, j: (l, 0, j)),
        ],
        out_specs=pl.BlockSpec((1, b, tn), lambda l, j: (l, 0, j)),
        compiler_params=_params(("arbitrary", "arbitrary")),
        name="adaln_mod",
    )(c, ada_w, ada_b.reshape(DEPTH, 1, n))


def _rope_table_kernel(pos_ref, freq_ref, o_ref):
    pos = pos_ref[0].astype(F32)
    ang = freq_ref[...] * pos
    cos = jnp.cos(ang)
    sin = jnp.sin(ang)
    tab = jnp.concatenate([cos, cos, -sin, sin], axis=0)
    o_ref[0] = tab.T


def _rope_table(positions):
    b, s = positions.shape
    t = min(s, 1024)
    inv_freq = ROPE_THETA ** (-jnp.arange(0, QK_ROPE_DIM, 2, dtype=F32) / QK_ROPE_DIM)
    return pl.pallas_call(
        _rope_table_kernel,
        out_shape=jax.ShapeDtypeStruct((b, s, LANES), F32),
        grid=(b, s // t),
        in_specs=[
            pl.BlockSpec((1, 1, t), lambda bi, i: (bi, 0, i)),
            pl.BlockSpec((ROPE_HALF, 1), lambda bi, i: (0, 0)),
        ],
        out_specs=pl.BlockSpec((1, t, LANES), lambda bi, i: (bi, i, 0)),
        compiler_params=_params(("arbitrary", "arbitrary")),
        name="rope_table",
    )(positions.reshape(b, 1, s), inv_freq.reshape(ROPE_HALF, 1))


def _pool_kernel(x_ref, xh_ref, shift_ref, scale_ref, gate_ref, w_ref, ps_ref, g_ref, b_ref,
                 o_ref, ext_ref):
    i = pl.program_id(1)
    t = x_ref.shape[1]
    x = x_ref[0]
    mul = 1.0 + scale_ref[0]
    shift = shift_ref[0]
    h = x * mul + shift
    hh = xh_ref[0] * mul + shift
    ext_ref[0:POOL_HALO, :] = jnp.where(i == 0, 0.0, hh)
    ext_ref[POOL_HALO:, :] = h

    row = i * t + lax.broadcasted_iota(jnp.int32, (t, 1), 0)
    ys = []
    for g, w in enumerate(POOL_WINDOWS):
        cols = slice(g * POOL_GROUP_DIM, (g + 1) * POOL_GROUP_DIM)
        acc = h[:, cols]
        for k in range(1, w):
            acc = acc + ext_ref[POOL_HALO - k:POOL_HALO - k + t, cols]
        cnt = jnp.minimum(row + 1, w).astype(F32)
        y = acc / cnt - h[:, cols]
        ys.append(jnp.dot(y.astype(BF16), w_ref[g], preferred_element_type=F32))
    y = jnp.concatenate(ys, axis=-1) * ps_ref[...]
    z = DEEPNORM_ALPHA * x + gate_ref[0] * y
    o_ref[0] = _layer_norm(z, g_ref[...], b_ref[...])


def _pool_layer(x, shift, scale, gate, w_pool, pool_scale, ln_g, ln_b):
    b, s, d = x.shape
    t = min(s, 512)
    halo_blocks = t // POOL_HALO
    mod_spec = pl.BlockSpec((1, 1, d), lambda bi, i: (bi, 0, 0))
    return pl.pallas_call(
        _pool_kernel,
        out_shape=jax.ShapeDtypeStruct((b, s, d), F32),
        grid=(b, s // t),
        in_specs=[
            pl.BlockSpec((1, t, d), lambda bi, i: (bi, i, 0)),
            pl.BlockSpec((1, POOL_HALO, d),
                         lambda bi, i: (bi, jnp.maximum(i * halo_blocks - 1, 0), 0)),
            mod_spec, mod_spec, mod_spec,
            _const_spec((N_POOL_GROUPS, POOL_GROUP_DIM, POOL_GROUP_DIM)),
            _const_spec((1, d)), _const_spec((1, d)), _const_spec((1, d)),
        ],
        out_specs=pl.BlockSpec((1, t, d), lambda bi, i: (bi, i, 0)),
        scratch_shapes=[pltpu.VMEM((POOL_HALO + t, d), F32)],
        compiler_params=_params(("arbitrary", "arbitrary")),
        name="pool_layer",
    )(x, x, shift, scale, gate, w_pool, pool_scale, ln_g, ln_b)


def _mlp_kernel(x_ref, shift_ref, scale_ref, gate_ref, w1_ref, w2_ref, g_ref, b_ref, o_ref,
                *, ff_chunk):
    x = x_ref[0]
    h = (x * (1.0 + scale_ref[0]) + shift_ref[0]).astype(BF16)
    y = None
    for c in range(D_FF // ff_chunk):
        cols = slice(c * ff_chunk, (c + 1) * ff_chunk)
        a = jnp.maximum(jnp.dot(h, w1_ref[:, cols], preferred_element_type=F32), 0.0)
        part = jnp.dot((a * a).astype(BF16), w2_ref[cols, :], preferred_element_type=F32)
        y = part if y is None else y + part
    z = DEEPNORM_ALPHA * x + gate_ref[0] * y
    o_ref[0] = _layer_norm(z, g_ref[...], b_ref[...])


def _mlp_layer(x, shift, scale, gate, w1, w2, ln_g, ln_b):
    b, s, d = x.shape
    t = min(s, 512)
    mod_spec = pl.BlockSpec((1, 1, d), lambda bi, i: (bi, 0, 0))
    return pl.pallas_call(
        functools.partial(_mlp_kernel, ff_chunk=1024),
        out_shape=jax.ShapeDtypeStruct((b, s, d), F32),
        grid=(b, s // t),
        in_specs=[
            pl.BlockSpec((1, t, d), lambda bi, i: (bi, i, 0)),
            mod_spec, mod_spec, mod_spec,
            _const_spec((d, D_FF)), _const_spec((D_FF, d)),
            _const_spec((1, d)), _const_spec((1, d)),
        ],
        out_specs=pl.BlockSpec((1, t, d), lambda bi, i: (bi, i, 0)),
        compiler_params=_params(("arbitrary", "arbitrary")),
        name="mlp_layer",
    )(x, shift, scale, gate, w1, w2, ln_g, ln_b)


def _kv_kernel(x_ref, tab_ref, wc_ref, wr_ref, ng_ref, kup_ref, vup_ref, k_ref, v_ref):
    xb = x_ref[0].astype(BF16)
    c = jnp.dot(xb, wc_ref[...], preferred_element_type=F32)
    ms = jnp.mean(c * c, axis=-1, keepdims=True)
    ckv = (c * lax.rsqrt(ms + RMS_EPS) * ng_ref[...]).astype(BF16)
    y = jnp.dot(xb, wr_ref[...], preferred_element_type=F32) * tab_ref[0]
    lane = lax.broadcasted_iota(jnp.int32, y.shape, 1)
    kr = jnp.where(lane < QK_ROPE_DIM, _rope_pair_sum(y), 0.0).astype(BF16)
    kn = jnp.dot(ckv, kup_ref[...], preferred_element_type=F32).astype(BF16)
    for h in range(N_HEADS):
        k_ref[0, :, h * HEAD_W:h * HEAD_W + LANES] = kn[:, h * LANES:(h + 1) * LANES]
        k_ref[0, :, h * HEAD_W + LANES:(h + 1) * HEAD_W] = kr
    v_ref[0] = jnp.dot(ckv, vup_ref[...], preferred_element_type=F32).astype(BF16)


def _kv_proj(x, tab, w_c, w_r2, norm_g, k_up, v_up):
    b, s, d = x.shape
    t = min(s, 512)
    return pl.pallas_call(
        _kv_kernel,
        out_shape=(jax.ShapeDtypeStruct((b, s, N_HEADS * HEAD_W), BF16),
                   jax.ShapeDtypeStruct((b, s, N_HEADS * V_HEAD_DIM), BF16)),
        grid=(b, s // t),
        in_specs=[
            pl.BlockSpec((1, t, d), lambda bi, i: (bi, i, 0)),
            pl.BlockSpec((1, t, LANES), lambda bi, i: (bi, i, 0)),
            _const_spec((d, KV_LORA_RANK)), _const_spec((d, LANES)),
            _const_spec((1, KV_LORA_RANK)),
            _const_spec((KV_LORA_RANK, N_HEADS * QK_NOPE_DIM)),
            _const_spec((KV_LORA_RANK, N_HEADS * V_HEAD_DIM)),
        ],
        out_specs=(pl.BlockSpec((1, t, N_HEADS * HEAD_W), lambda bi, i: (bi, i, 0)),
                   pl.BlockSpec((1, t, N_HEADS * V_HEAD_DIM), lambda bi, i: (bi, i, 0))),
        compiler_params=_params(("arbitrary", "arbitrary")),
        name="kv_proj",
    )(x, tab, w_c, w_r2, norm_g, k_up, v_up)


def _q_kernel(x_ref, tab_ref, shift_ref, scale_ref, wd_ref, ng_ref, wn_ref, wr_ref, q_ref):
    h = (x_ref[0] * (1.0 + scale_ref[0]) + shift_ref[0]).astype(BF16)
    c = jnp.dot(h, wd_ref[...], preferred_element_type=F32)
    ms = jnp.mean(c * c, axis=-1, keepdims=True)
    cq = (c * lax.rsqrt(ms + RMS_EPS) * ng_ref[...]).astype(BF16)
    qn = jnp.dot(cq, wn_ref[...], preferred_element_type=F32)
    qr = jnp.dot(cq, wr_ref[...], preferred_element_type=F32)
    tab = tab_ref[0]
    for hd in range(N_HEADS):
        lanes = slice(hd * LANES, (hd + 1) * LANES)
        q_ref[0, :, hd * HEAD_W:hd * HEAD_W + LANES] = (qn[:, lanes] * ATTN_SCALE).astype(BF16)
        rot = _rope_pair_sum(qr[:, lanes] * tab)
        q_ref[0, :, hd * HEAD_W + LANES:(hd + 1) * HEAD_W] = (rot * ATTN_SCALE).astype(BF16)


def _q_proj(x, tab, shift, scale, w_down, norm_g, w_nope, w_rope2):
    b, s, d = x.shape
    t = min(s, 512)
    mod_spec = pl.BlockSpec((1, 1, d), lambda bi, i: (bi, 0, 0))
    return pl.pallas_call(
        _q_kernel,
        out_shape=jax.ShapeDtypeStruct((b, s, N_HEADS * HEAD_W), BF16),
        grid=(b, s // t),
        in_specs=[
            pl.BlockSpec((1, t, d), lambda bi, i: (bi, i, 0)),
            pl.BlockSpec((1, t, LANES), lambda bi, i: (bi, i, 0)),
            mod_spec, mod_spec,
            _const_spec((d, Q_LORA_RANK)), _const_spec((1, Q_LORA_RANK)),
            _const_spec((Q_LORA_RANK, N_HEADS * LANES)),
            _const_spec((Q_LORA_RANK, N_HEADS * LANES)),
        ],
        out_specs=pl.BlockSpec((1, t, N_HEADS * HEAD_W), lambda bi, i: (bi, i, 0)),
        compiler_params=_params(("arbitrary", "arbitrary")),
        name="q_proj",
    )(x, tab, shift, scale, w_down, norm_g, w_nope, w_rope2)


def _attn_kernel(qi_ref, kj_ref, q_ref, k_ref, v_ref, x_ref, gate_ref, wo_ref, g_ref, b_ref,
                 o_ref, m_ref, l_ref, acc_ref):
    p = pl.program_id(1)
    i = qi_ref[p]
    j = kj_ref[p]
    tq = q_ref.shape[1]
    tk = k_ref.shape[1]

    @pl.when(j == 0)
    def _():
        m_ref[...] = jnp.full(m_ref.shape, -jnp.inf, F32)
        l_ref[...] = jnp.zeros(l_ref.shape, F32)
        acc_ref[...] = jnp.zeros(acc_ref.shape, F32)

    def step(diagonal):
        if diagonal:
            keep = (lax.broadcasted_iota(jnp.int32, (tq, tk), 1)
                    <= lax.broadcasted_iota(jnp.int32, (tq, tk), 0))
        for h in range(N_HEADS):
            q = q_ref[0, :, h * HEAD_W:(h + 1) * HEAD_W]
            k = k_ref[0, :, h * HEAD_W:(h + 1) * HEAD_W]
            s = lax.dot_general(q, k, (((1,), (1,)), ((), ())), preferred_element_type=F32)
            if diagonal:
                s = jnp.where(keep, s, -jnp.inf)
            m_prev = m_ref[h]
            m_new = jnp.maximum(m_prev, jnp.max(s, axis=-1, keepdims=True))
            alpha = jnp.exp(m_prev - m_new)
            e = jnp.exp(s - m_new)
            l_ref[h] = alpha * l_ref[h] + jnp.sum(e, axis=-1, keepdims=True)
            m_ref[h] = m_new
            pv = jnp.dot(e.astype(BF16), v_ref[0, :, h * V_HEAD_DIM:(h + 1) * V_HEAD_DIM],
                         preferred_element_type=F32)
            acc_ref[h] = alpha * acc_ref[h] + pv

    @pl.when(j < i)
    def _():
        step(False)

    @pl.when(j == i)
    def _():
        step(True)
        o = jnp.concatenate([(acc_ref[h] / l_ref[h]).astype(BF16) for h in range(N_HEADS)],
                            axis=-1)
        y = jnp.dot(o, wo_ref[...], preferred_element_type=F32)
        z = DEEPNORM_ALPHA * x_ref[0] + gate_ref[0] * y
        o_ref[0] = _layer_norm(z, g_ref[...], b_ref[...])


def _attn_layer(x, q, k, v, gate, w_out, ln_g, ln_b):
    b, s, d = x.shape
    t = min(s, 512)
    nq = s // t
    pairs = [(i, j) for i in range(nq) for j in range(i + 1)]
    qi = jnp.asarray(np.array([p[0] for p in pairs], np.int32))
    kj = jnp.asarray(np.array([p[1] for p in pairs], np.int32))
    grid_spec = pltpu.PrefetchScalarGridSpec(
        num_scalar_prefetch=2,
        grid=(b, len(pairs)),
        in_specs=[
            pl.BlockSpec((1, t, N_HEADS * HEAD_W), lambda bi, p, qi, kj: (bi, qi[p], 0)),
            pl.BlockSpec((1, t, N_HEADS * HEAD_W), lambda bi, p, qi, kj: (bi, kj[p], 0)),
            pl.BlockSpec((1, t, N_HEADS * V_HEAD_DIM), lambda bi, p, qi, kj: (bi, kj[p], 0)),
            pl.BlockSpec((1, t, d), lambda bi, p, qi, kj: (bi, qi[p], 0)),
            pl.BlockSpec((1, 1, d), lambda bi, p, qi, kj: (bi, 0, 0)),
            _const_spec((d, d)), _const_spec((1, d)), _const_spec((1, d)),
        ],
        out_specs=pl.BlockSpec((1, t, d), lambda bi, p, qi, kj: (bi, qi[p], 0)),
        scratch_shapes=[
            pltpu.VMEM((N_HEADS, t, 1), F32),
            pltpu.VMEM((N_HEADS, t, 1), F32),
            pltpu.VMEM((N_HEADS, t, V_HEAD_DIM), F32),
        ],
    )
    return pl.pallas_call(
        _attn_kernel,
        out_shape=jax.ShapeDtypeStruct((b, s, d), F32),
        grid_spec=grid_spec,
        compiler_params=_params(("arbitrary", "arbitrary")),
        name="attn_layer",
    )(qi, kj, q, k, v, x, gate, w_out, ln_g, ln_b)


def _swap_halves(w):
    half = w.shape[-1] // 2
    return jnp.concatenate([w[..., half:], w[..., :half]], axis=-1)


def _split_q_up(w):
    r = w.shape[0]
    w = w.reshape(r, N_HEADS, QK_NOPE_DIM + QK_ROPE_DIM)
    nope = w[..., :QK_NOPE_DIM].reshape(r, N_HEADS * QK_NOPE_DIM)
    rope = w[..., QK_NOPE_DIM:]
    rope2 = jnp.concatenate([rope, _swap_halves(rope)], axis=-1).reshape(r, N_HEADS * LANES)
    return nope.astype(BF16), rope2.astype(BF16)


def kernel(x, c, positions, ada_w, ada_b, ln_g, ln_b, mlp_w1, mlp_w2, pool_w, pool_scale,
           q_down_w, q_norm_g, q_up_w, attn_out_w, kv_in_w, kv_norm_g, k_up_w, v_up_w):
    b, s, d = x.shape
    mod = _adaln_mod(c, ada_w, ada_b).reshape(DEPTH, b, N_MOD, 1, d)
    tab = _rope_table(positions)

    kv_rope_w = kv_in_w[:, KV_LORA_RANK:]
    kv_w_c = kv_in_w[:, :KV_LORA_RANK].astype(BF16)
    kv_w_r2 = jnp.concatenate([kv_rope_w, _swap_halves(kv_rope_w)], axis=-1).astype(BF16)

    k = v = None
    for l in range(DEPTH):
        shift1, scale1, gate1, shift2, scale2, gate2 = (mod[l, :, m] for m in range(N_MOD))
        g1, b1 = ln_g[l, 0].reshape(1, d), ln_b[l, 0].reshape(1, d)
        g2, b2 = ln_g[l, 1].reshape(1, d), ln_b[l, 1].reshape(1, d)
        if l < N_A_LAYERS:
            x = _pool_layer(x, shift1, scale1, gate1, pool_w[l].astype(BF16),
                            pool_scale[l].reshape(1, d), g1, b1)
        else:
            if l == N_A_LAYERS:
                k, v = _kv_proj(x, tab, kv_w_c, kv_w_r2, kv_norm_g.reshape(1, KV_LORA_RANK),
                                k_up_w.astype(BF16), v_up_w.astype(BF16))
            jl = l - N_A_LAYERS
            w_nope, w_rope2 = _split_q_up(q_up_w[jl])
            q = _q_proj(x, tab, shift1, scale1, q_down_w[jl].astype(BF16),
                        q_norm_g[jl].reshape(1, Q_LORA_RANK), w_nope, w_rope2)
            x = _attn_layer(x, q, k, v, gate1, attn_out_w[jl].astype(BF16), g1, b1)
        x = _mlp_layer(x, shift2, scale2, gate2, mlp_w1[l].astype(BF16), mlp_w2[l].astype(BF16),
                       g2, b2)
    return x
```

```python
import functools

import numpy as np
import jax
import jax.numpy as jnp
from jax import lax
from jax.experimental import pallas as pl
from jax.experimental.pallas import tpu as pltpu

D_MODEL = 1024
DEPTH = 4
N_A_LAYERS = DEPTH // 2
POOL_WINDOWS = (2, 4, 8, 16)
N_POOL_GROUPS = len(POOL_WINDOWS)
POOL_GROUP_DIM = D_MODEL // N_POOL_GROUPS
POOL_HALO = 16
QK_NOPE_DIM = 128
QK_ROPE_DIM = 64
V_HEAD_DIM = 128
N_HEADS = D_MODEL // 128
Q_LORA_RANK = D_MODEL // 2
KV_LORA_RANK = D_MODEL // 4
ROPE_THETA = 10000.0
D_FF = 4 * D_MODEL
N_MOD = 6
DEEPNORM_ALPHA = (2.0 * DEPTH) ** 0.25
LN_EPS = 1e-5
RMS_EPS = 1e-6
ATTN_SCALE = (QK_NOPE_DIM + QK_ROPE_DIM) ** -0.5

LANES = 128
HEAD_W = 2 * LANES
ROPE_HALF = QK_ROPE_DIM // 2

VMEM_LIMIT = 56 * 1024 * 1024

F32 = jnp.float32
BF16 = jnp.bfloat16

_NT_DIMS = (((1,), (1,)), ((), ()))


def _params(semantics):
    return pltpu.CompilerParams(dimension_semantics=semantics, vmem_limit_bytes=VMEM_LIMIT)


def _const_spec(shape):
    nd = len(shape)
    return pl.BlockSpec(shape, lambda *_: (0,) * nd, pipeline_mode=pl.Buffered(1))


def _layer_norm(z, g, b):
    mu = jnp.mean(z, axis=-1, keepdims=True)
    zc = z - mu
    var = jnp.mean(zc * zc, axis=-1, keepdims=True)
    return zc * lax.rsqrt(var + LN_EPS) * g + b


def _rope_pair_sum(y):
    return y + pltpu.roll(y, QK_ROPE_DIM, axis=1)


def _adaln_kernel(c_ref, w_ref, b_ref, o_ref):
    c = c_ref[...]
    sc = c * jax.nn.sigmoid(c)
    o_ref[0] = jnp.dot(sc.astype(BF16), w_ref[0].astype(BF16),
                       preferred_element_type=F32) + b_ref[0]


def _adaln_mod(c, ada_w, ada_b):
    b, d = c.shape
    n = ada_w.shape[-1]
    tn = 1536
    return pl.pallas_call(
        _adaln_kernel,
        out_shape=jax.ShapeDtypeStruct((DEPTH, b, n), F32),
        grid=(DEPTH, n // tn),
        in_specs=[
            pl.BlockSpec((b, d), lambda l, j: (0, 0)),
            pl.BlockSpec((1, d, tn), lambda l, j: (l, 0, j)),
            pl.BlockSpec((1, 1, tn), lambda l, j: (l, 0, j)),
        ],
        out_specs=pl.BlockSpec((1, b, tn), lambda l, j: (l, 0, j)),
        compiler_params=_params(("arbitrary", "arbitrary")),
        name="adaln_mod",
    )(c, ada_w, ada_b.reshape(DEPTH, 1, n))


def _rope_table_kernel(pos_ref, freq_ref, o_ref, ot_ref):
    pos = pos_ref[0].astype(F32)
    ang = freq_ref[...] * pos
    cos = jnp.cos(ang)
    sin = jnp.sin(ang)
    tab = jnp.concatenate([cos, cos, -sin, sin], axis=0)
    ot_ref[0] = tab
    o_ref[0] = tab.T


def _rope_table(positions):
    b, s = positions.shape
    t = min(s, 1024)
    inv_freq = ROPE_THETA ** (-jnp.arange(0, QK_ROPE_DIM, 2, dtype=F32) / QK_ROPE_DIM)
    return pl.pallas_call(
        _rope_table_kernel,
        out_shape=(jax.ShapeDtypeStruct((b, s, LANES), F32),
                   jax.ShapeDtypeStruct((b, LANES, s), F32)),
        grid=(b, s // t),
        in_specs=[
            pl.BlockSpec((1, 1, t), lambda bi, i: (bi, 0, i)),
            pl.BlockSpec((ROPE_HALF, 1), lambda bi, i: (0, 0)),
        ],
        out_specs=(pl.BlockSpec((1, t, LANES), lambda bi, i: (bi, i, 0)),
                   pl.BlockSpec((1, LANES, t), lambda bi, i: (bi, 0, i))),
        compiler_params=_params(("arbitrary", "arbitrary")),
        name="rope_table",
    )(positions.reshape(b, 1, s), inv_freq.reshape(ROPE_HALF, 1))


def _pool_kernel(x_ref, xh_ref, shift_ref, scale_ref, gate_ref, w_ref, ps_ref, g_ref, b_ref,
                 o_ref, ext_ref):
    i = pl.program_id(1)
    t = x_ref.shape[1]
    x = x_ref[0]
    mul = 1.0 + scale_ref[0]
    shift = shift_ref[0]
    h = x * mul + shift
    hh = xh_ref[0] * mul + shift
    ext_ref[0:POOL_HALO, :] = jnp.where(i == 0, 0.0, hh)
    ext_ref[POOL_HALO:, :] = h

    row = i * t + lax.broadcasted_iota(jnp.int32, (t, 1), 0)
    ys = []
    for g, w in enumerate(POOL_WINDOWS):
        cols = slice(g * POOL_GROUP_DIM, (g + 1) * POOL_GROUP_DIM)
        acc = h[:, cols]
        for k in range(1, w):
            acc = acc + ext_ref[POOL_HALO - k:POOL_HALO - k + t, cols]
        cnt = jnp.minimum(row + 1, w).astype(F32)
        y = acc / cnt - h[:, cols]
        ys.append(jnp.dot(y.astype(BF16), w_ref[g], preferred_element_type=F32))
    y = jnp.concatenate(ys, axis=-1) * ps_ref[...]
    z = DEEPNORM_ALPHA * x + gate_ref[0] * y
    o_ref[0] = _layer_norm(z, g_ref[...], b_ref[...])


def _pool_layer(x, shift, scale, gate, w_pool, pool_scale, ln_g, ln_b):
    b, s, d = x.shape
    t = min(s, 512)
    halo_blocks = t // POOL_HALO
    mod_spec = pl.BlockSpec((1, 1, d), lambda bi, i: (bi, 0, 0))
    return pl.pallas_call(
        _pool_kernel,
        out_shape=jax.ShapeDtypeStruct((b, s, d), F32),
        grid=(b, s // t),
        in_specs=[
            pl.BlockSpec((1, t, d), lambda bi, i: (bi, i, 0)),
            pl.BlockSpec((1, POOL_HALO, d),
                         lambda bi, i: (bi, jnp.maximum(i * halo_blocks - 1, 0), 0)),
            mod_spec, mod_spec, mod_spec,
            _const_spec((N_POOL_GROUPS, POOL_GROUP_DIM, POOL_GROUP_DIM)),
            _const_spec((1, d)), _const_spec((1, d)), _const_spec((1, d)),
        ],
        out_specs=pl.BlockSpec((1, t, d), lambda bi, i: (bi, i, 0)),
        scratch_shapes=[pltpu.VMEM((POOL_HALO + t, d), F32)],
        compiler_params=_params(("arbitrary", "arbitrary")),
        name="pool_layer",
    )(x, x, shift, scale, gate, w_pool, pool_scale, ln_g, ln_b)


def _mlp_kernel(x_ref, shift_ref, scale_ref, gate_ref, w1_ref, w2_ref, g_ref, b_ref, o_ref,
                *, ff_chunk):
    x = x_ref[0]
    h = (x * (1.0 + scale_ref[0]) + shift_ref[0]).astype(BF16)
    y = None
    for c in range(D_FF // ff_chunk):
        cols = slice(c * ff_chunk, (c + 1) * ff_chunk)
        a = jnp.maximum(jnp.dot(h, w1_ref[:, cols], preferred_element_type=F32), 0.0)
        part = jnp.dot((a * a).astype(BF16), w2_ref[cols, :], preferred_element_type=F32)
        y = part if y is None else y + part
    z = DEEPNORM_ALPHA * x + gate_ref[0] * y
    o_ref[0] = _layer_norm(z, g_ref[...], b_ref[...])


def _mlp_layer(x, shift, scale, gate, w1, w2, ln_g, ln_b):
    b, s, d = x.shape
    t = min(s, 512)
    mod_spec = pl.BlockSpec((1, 1, d), lambda bi, i: (bi, 0, 0))
    return pl.pallas_call(
        functools.partial(_mlp_kernel, ff_chunk=1024),
        out_shape=jax.ShapeDtypeStruct((b, s, d), F32),
        grid=(b, s // t),
        in_specs=[
            pl.BlockSpec((1, t, d), lambda bi, i: (bi, i, 0)),
            mod_spec, mod_spec, mod_spec,
            _const_spec((d, D_FF)), _const_spec((D_FF, d)),
            _const_spec((1, d)), _const_spec((1, d)),
        ],
        out_specs=pl.BlockSpec((1, t, d), lambda bi, i: (bi, i, 0)),
        compiler_params=_params(("arbitrary", "arbitrary")),
        name="mlp_layer",
    )(x, shift, scale, gate, w1, w2, ln_g, ln_b)


def _kv_kernel(x_ref, tab_ref, wc_ref, wr_ref, ng_ref, kup_ref, vup_ref, k_ref, vt_ref):
    xb = x_ref[0].astype(BF16)
    c = jnp.dot(xb, wc_ref[...], preferred_element_type=F32)
    ms = jnp.mean(c * c, axis=-1, keepdims=True)
    ckv = (c * lax.rsqrt(ms + RMS_EPS) * ng_ref[...]).astype(BF16)
    y = jnp.dot(xb, wr_ref[...], preferred_element_type=F32) * tab_ref[0]
    lane = lax.broadcasted_iota(jnp.int32, y.shape, 1)
    kr = jnp.where(lane < QK_ROPE_DIM, _rope_pair_sum(y), 0.0).astype(BF16)
    kn = jnp.dot(ckv, kup_ref[...], preferred_element_type=F32).astype(BF16)
    for h in range(N_HEADS):
        k_ref[0, :, h * HEAD_W:h * HEAD_W + LANES] = kn[:, h * LANES:(h + 1) * LANES]
        k_ref[0, :, h * HEAD_W + LANES:(h + 1) * HEAD_W] = kr
    vt_ref[0] = lax.dot_general(vup_ref[...], ckv, _NT_DIMS,
                                preferred_element_type=F32).astype(BF16)


def _kv_proj(x, tab, w_c, w_r2, norm_g, k_up, v_up_t):
    b, s, d = x.shape
    t = min(s, 512)
    return pl.pallas_call(
        _kv_kernel,
        out_shape=(jax.ShapeDtypeStruct((b, s, N_HEADS * HEAD_W), BF16),
                   jax.ShapeDtypeStruct((b, N_HEADS * V_HEAD_DIM, s), BF16)),
        grid=(b, s // t),
        in_specs=[
            pl.BlockSpec((1, t, d), lambda bi, i: (bi, i, 0)),
            pl.BlockSpec((1, t, LANES), lambda bi, i: (bi, i, 0)),
            _const_spec((d, KV_LORA_RANK)), _const_spec((d, LANES)),
            _const_spec((1, KV_LORA_RANK)),
            _const_spec((KV_LORA_RANK, N_HEADS * QK_NOPE_DIM)),
            _const_spec((N_HEADS * V_HEAD_DIM, KV_LORA_RANK)),
        ],
        out_specs=(pl.BlockSpec((1, t, N_HEADS * HEAD_W), lambda bi, i: (bi, i, 0)),
                   pl.BlockSpec((1, N_HEADS * V_HEAD_DIM, t), lambda bi, i: (bi, 0, i))),
        compiler_params=_params(("arbitrary", "arbitrary")),
        name="kv_proj",
    )(x, tab, w_c, w_r2, norm_g, k_up, v_up_t)


def _q_kernel(x_ref, tabt_ref, shift_ref, scale_ref, wd_ref, ng_ref, wn_ref, wr_ref, qt_ref):
    h = (x_ref[0] * (1.0 + scale_ref[0]) + shift_ref[0]).astype(BF16)
    c = jnp.dot(h, wd_ref[...], preferred_element_type=F32)
    ms = jnp.mean(c * c, axis=-1, keepdims=True)
    cq = (c * lax.rsqrt(ms + RMS_EPS) * ng_ref[...]).astype(BF16)
    qn = lax.dot_general(wn_ref[...], cq, _NT_DIMS, preferred_element_type=F32)
    qr = lax.dot_general(wr_ref[...], cq, _NT_DIMS, preferred_element_type=F32)
    tabt = tabt_ref[0]
    for hd in range(N_HEADS):
        rows = slice(hd * LANES, (hd + 1) * LANES)
        qt_ref[0, hd * HEAD_W:hd * HEAD_W + LANES, :] = (qn[rows] * ATTN_SCALE).astype(BF16)
        y = qr[rows] * tabt
        rot = ((y[:QK_ROPE_DIM] + y[QK_ROPE_DIM:]) * ATTN_SCALE).astype(BF16)
        qt_ref[0, hd * HEAD_W + LANES:hd * HEAD_W + LANES + QK_ROPE_DIM, :] = rot
        qt_ref[0, hd * HEAD_W + LANES + QK_ROPE_DIM:(hd + 1) * HEAD_W, :] = rot


def _q_proj(x, tabt, shift, scale, w_down, norm_g, w_nope_t, w_rope2_t):
    b, s, d = x.shape
    t = min(s, 512)
    mod_spec = pl.BlockSpec((1, 1, d), lambda bi, i: (bi, 0, 0))
    return pl.pallas_call(
        _q_kernel,
        out_shape=jax.ShapeDtypeStruct((b, N_HEADS * HEAD_W, s), BF16),
        grid=(b, s // t),
        in_specs=[
            pl.BlockSpec((1, t, d), lambda bi, i: (bi, i, 0)),
            pl.BlockSpec((1, LANES, t), lambda bi, i: (bi, 0, i)),
            mod_spec, mod_spec,
            _const_spec((d, Q_LORA_RANK)), _const_spec((1, Q_LORA_RANK)),
            _const_spec((N_HEADS * LANES, Q_LORA_RANK)),
            _const_spec((N_HEADS * LANES, Q_LORA_RANK)),
        ],
        out_specs=pl.BlockSpec((1, N_HEADS * HEAD_W, t), lambda bi, i: (bi, 0, i)),
        compiler_params=_params(("arbitrary", "arbitrary")),
        name="q_proj",
    )(x, tabt, shift, scale, w_down, norm_g, w_nope_t, w_rope2_t)


def _attn_kernel(qi_ref, kj_ref, qt_ref, k_ref, vt_ref, x_ref, gate_ref, wo_ref, g_ref, b_ref,
                 o_ref, m_ref, l_ref, acc_ref):
    p = pl.program_id(1)
    i = qi_ref[p]
    j = kj_ref[p]
    tq = qt_ref.shape[2]
    tk = k_ref.shape[1]

    @pl.when(j == 0)
    def _():
        m_ref[...] = jnp.full(m_ref.shape, -jnp.inf, F32)
        l_ref[...] = jnp.zeros(l_ref.shape, F32)
        acc_ref[...] = jnp.zeros(acc_ref.shape, F32)

    def step(diagonal):
        if diagonal:
            keep = (lax.broadcasted_iota(jnp.int32, (tk, tq), 0)
                    <= lax.broadcasted_iota(jnp.int32, (tk, tq), 1))
        for h in range(N_HEADS):
            rows = slice(h * V_HEAD_DIM, (h + 1) * V_HEAD_DIM)
            st = jnp.dot(k_ref[0, :, h * HEAD_W:(h + 1) * HEAD_W],
                         qt_ref[0, h * HEAD_W:(h + 1) * HEAD_W, :],
                         preferred_element_type=F32)
            if diagonal:
                st = jnp.where(keep, st, -jnp.inf)
            m_prev = m_ref[h]
            m_new = jnp.maximum(m_prev, jnp.max(st, axis=0, keepdims=True))
            alpha = jnp.exp(m_prev - m_new)
            e = jnp.exp(st - m_new)
            l_ref[h] = alpha * l_ref[h] + jnp.sum(e, axis=0, keepdims=True)
            m_ref[h] = m_new
            pv = jnp.dot(vt_ref[0, rows, :], e.astype(BF16),
                         preferred_element_type=F32)
            acc_ref[rows, :] = alpha * acc_ref[rows, :] + pv

    @pl.when(j < i)
    def _():
        step(False)

    @pl.when(j == i)
    def _():
        step(True)
        o = jnp.concatenate(
            [(acc_ref[h * V_HEAD_DIM:(h + 1) * V_HEAD_DIM, :] / l_ref[h]).T.astype(BF16)
             for h in range(N_HEADS)], axis=-1)
        y = jnp.dot(o, wo_ref[...], preferred_element_type=F32)
        z = DEEPNORM_ALPHA * x_ref[0] + gate_ref[0] * y
        o_ref[0] = _layer_norm(z, g_ref[...], b_ref[...])


def _attn_layer(x, qt, k, vt, gate, w_out, ln_g, ln_b):
    b, s, d = x.shape
    t = min(s, 512)
    nq = s // t
    pairs = [(i, j) for i in range(nq) for j in range(i + 1)]
    qi = jnp.asarray(np.array([p[0] for p in pairs], np.int32))
    kj = jnp.asarray(np.array([p[1] for p in pairs], np.int32))
    grid_spec = pltpu.PrefetchScalarGridSpec(
        num_scalar_prefetch=2,
        grid=(b, len(pairs)),
        in_specs=[
            pl.BlockSpec((1, N_HEADS * HEAD_W, t), lambda bi, p, qi, kj: (bi, 0, qi[p])),
            pl.BlockSpec((1, t, N_HEADS * HEAD_W), lambda bi, p, qi, kj: (bi, kj[p], 0)),
            pl.BlockSpec((1, N_HEADS * V_HEAD_DIM, t), lambda bi, p, qi, kj: (bi, 0, kj[p])),
            pl.BlockSpec((1, t, d), lambda bi, p, qi, kj: (bi, qi[p], 0)),
            pl.BlockSpec((1, 1, d), lambda bi, p, qi, kj: (bi, 0, 0)),
            _const_spec((d, d)), _const_spec((1, d)), _const_spec((1, d)),
        ],
        out_specs=pl.BlockSpec((1, t, d), lambda bi, p, qi, kj: (bi, qi[p], 0)),
        scratch_shapes=[
            pltpu.VMEM((N_HEADS, 1, t), F32),
            pltpu.VMEM((N_HEADS, 1, t), F32),
            pltpu.VMEM((N_HEADS * V_HEAD_DIM, t), F32),
        ],
    )
    return pl.pallas_call(
        _attn_kernel,
        out_shape=jax.ShapeDtypeStruct((b, s, d), F32),
        grid_spec=grid_spec,
        compiler_params=_params(("arbitrary", "arbitrary")),
        name="attn_layer",
    )(qi, kj, qt, k, vt, x, gate, w_out, ln_g, ln_b)


def _swap_halves(w):
    half = w.shape[-1] // 2
    return jnp.concatenate([w[..., half:], w[..., :half]], axis=-1)


def _split_q_up(w):
    r = w.shape[0]
    w = w.reshape(r, N_HEADS, QK_NOPE_DIM + QK_ROPE_DIM)
    nope = w[..., :QK_NOPE_DIM].reshape(r, N_HEADS * QK_NOPE_DIM)
    rope = w[..., QK_NOPE_DIM:]
    rope2 = jnp.concatenate([rope, _swap_halves(rope)], axis=-1).reshape(r, N_HEADS * LANES)
    return nope.T.astype(BF16), rope2.T.astype(BF16)


def kernel(x, c, positions, ada_w, ada_b, ln_g, ln_b, mlp_w1, mlp_w2, pool_w, pool_scale,
           q_down_w, q_norm_g, q_up_w, attn_out_w, kv_in_w, kv_norm_g, k_up_w, v_up_w):
    b, s, d = x.shape
    mod = _adaln_mod(c, ada_w, ada_b).reshape(DEPTH, b, N_MOD, 1, d)
    tab, tabt = _rope_table(positions)

    kv_rope_w = kv_in_w[:, KV_LORA_RANK:]
    kv_w_c = kv_in_w[:, :KV_LORA_RANK].astype(BF16)
    kv_w_r2 = jnp.concatenate([kv_rope_w, _swap_halves(kv_rope_w)], axis=-1).astype(BF16)

    k = vt = None
    for l in range(DEPTH):
        shift1, scale1, gate1, shift2, scale2, gate2 = (mod[l, :, m] for m in range(N_MOD))
        g1, b1 = ln_g[l, 0].reshape(1, d), ln_b[l, 0].reshape(1, d)
        g2, b2 = ln_g[l, 1].reshape(1, d), ln_b[l, 1].reshape(1, d)
        if l < N_A_LAYERS:
            x = _pool_layer(x, shift1, scale1, gate1, pool_w[l].astype(BF16),
                            pool_scale[l].reshape(1, d), g1, b1)
        else:
            if l == N_A_LAYERS:
                k, vt = _kv_proj(x, tab, kv_w_c, kv_w_r2, kv_norm_g.reshape(1, KV_LORA_RANK),
                                 k_up_w.astype(BF16), v_up_w.T.astype(BF16))
            jl = l - N_A_LAYERS
            w_nope_t, w_rope2_t = _split_q_up(q_up_w[jl])
            qt = _q_proj(x, tabt, shift1, scale1, q_down_w[jl].astype(BF16),
                         q_norm_g[jl].reshape(1, Q_LORA_RANK), w_nope_t, w_rope2_t)
            x = _attn_layer(x, qt, k, vt, gate1, attn_out_w[jl].astype(BF16), g1, b1)
        x = _mlp_layer(x, shift2, scale2, gate2, mlp_w1[l].astype(BF16), mlp_w2[l].astype(BF16),
                       g2, b2)
    return x
```

```python
import functools

import numpy as np
import jax
import jax.numpy as jnp
from jax import lax
from jax.experimental import pallas as pl
from jax.experimental.pallas import tpu as pltpu

D_MODEL = 1024
DEPTH = 4
N_A_LAYERS = DEPTH // 2
POOL_WINDOWS = (2, 4, 8, 16)
N_POOL_GROUPS = len(POOL_WINDOWS)
POOL_GROUP_DIM = D_MODEL // N_POOL_GROUPS
POOL_HALO = 16
QK_NOPE_DIM = 128
QK_ROPE_DIM = 64
V_HEAD_DIM = 128
N_HEADS = D_MODEL // 128
Q_LORA_RANK = D_MODEL // 2
KV_LORA_RANK = D_MODEL // 4
ROPE_THETA = 10000.0
D_FF = 4 * D_MODEL
N_MOD = 6
DEEPNORM_ALPHA = (2.0 * DEPTH) ** 0.25
LN_EPS = 1e-5
RMS_EPS = 1e-6
ATTN_SCALE = (QK_NOPE_DIM + QK_ROPE_DIM) ** -0.5
Q_SCALE = ATTN_SCALE * float(np.log2(np.e))

LANES = 128
BF16_SUBLANES = 16
STRIP = 512
HEAD_W = 2 * LANES
ROPE_HALF = QK_ROPE_DIM // 2

VMEM_LIMIT = 56 * 1024 * 1024

F32 = jnp.float32
BF16 = jnp.bfloat16

_NT_DIMS = (((1,), (1,)), ((), ()))


def _params(semantics):
    return pltpu.CompilerParams(dimension_semantics=semantics, vmem_limit_bytes=VMEM_LIMIT)


def _const_spec(shape):
    nd = len(shape)
    return pl.BlockSpec(shape, lambda *_: (0,) * nd, pipeline_mode=pl.Buffered(1))


def _layer_norm(z, g, b):
    mu = jnp.mean(z, axis=-1, keepdims=True)
    zc = z - mu
    var = jnp.mean(zc * zc, axis=-1, keepdims=True)
    return zc * lax.rsqrt(var + LN_EPS) * g + b


def _rope_pair_sum(y):
    return y + pltpu.roll(y, QK_ROPE_DIM, axis=1)


def _adaln_kernel(c_ref, w_ref, b_ref, o_ref):
    c = c_ref[...]
    sc = c * jax.nn.sigmoid(c)
    o_ref[0] = jnp.dot(sc.astype(BF16), w_ref[0].astype(BF16),
                       preferred_element_type=F32) + b_ref[0]


def _adaln_mod(c, ada_w, ada_b):
    b, d = c.shape
    n = ada_w.shape[-1]
    tn = 1536
    return pl.pallas_call(
        _adaln_kernel,
        out_shape=jax.ShapeDtypeStruct((DEPTH, b, n), F32),
        grid=(DEPTH, n // tn),
        in_specs=[
            pl.BlockSpec((b, d), lambda l, j: (0, 0)),
            pl.BlockSpec((1, d, tn), lambda l, j: (l, 0, j)),
            pl.BlockSpec((1, 1, tn), lambda l, j: (l, 0, j)),
        ],
        out_specs=pl.BlockSpec((1, b, tn), lambda l, j: (l, 0, j)),
        compiler_params=_params(("arbitrary", "arbitrary")),
        name="adaln_mod",
    )(c, ada_w, ada_b.reshape(DEPTH, 1, n))


def _rope_table_kernel(pos_ref, freq_ref, o_ref, ot_ref):
    pos = pos_ref[0].astype(F32)
    ang = freq_ref[...] * pos
    cos = jnp.cos(ang)
    sin = jnp.sin(ang)
    tab = jnp.concatenate([cos, cos, -sin, sin], axis=0)
    ot_ref[0] = tab
    o_ref[0] = tab.T


def _rope_table(positions):
    b, s = positions.shape
    t = min(s, 1024)
    inv_freq = ROPE_THETA ** (-jnp.arange(0, QK_ROPE_DIM, 2, dtype=F32) / QK_ROPE_DIM)
    return pl.pallas_call(
        _rope_table_kernel,
        out_shape=(jax.ShapeDtypeStruct((b, s, LANES), F32),
                   jax.ShapeDtypeStruct((b, LANES, s), F32)),
        grid=(b, s // t),
        in_specs=[
            pl.BlockSpec((1, 1, t), lambda bi, i: (bi, 0, i)),
            pl.BlockSpec((ROPE_HALF, 1), lambda bi, i: (0, 0)),
        ],
        out_specs=(pl.BlockSpec((1, t, LANES), lambda bi, i: (bi, i, 0)),
                   pl.BlockSpec((1, LANES, t), lambda bi, i: (bi, 0, i))),
        compiler_params=_params(("arbitrary", "arbitrary")),
        name="rope_table",
    )(positions.reshape(b, 1, s), inv_freq.reshape(ROPE_HALF, 1))


def _pool_kernel(x_ref, xh_ref, shift_ref, scale_ref, gate_ref, w_ref, ps_ref, g_ref, b_ref,
                 o_ref, ext_ref):
    i = pl.program_id(1)
    t = x_ref.shape[1]
    x = x_ref[0]
    mul = 1.0 + scale_ref[0]
    shift = shift_ref[0]
    h = x * mul + shift
    hh = xh_ref[0] * mul + shift
    ext_ref[0:POOL_HALO, :] = jnp.where(i == 0, 0.0, hh)
    ext_ref[POOL_HALO:, :] = h

    row = i * t + lax.broadcasted_iota(jnp.int32, (t, 1), 0)
    ys = []
    for g, w in enumerate(POOL_WINDOWS):
        cols = slice(g * POOL_GROUP_DIM, (g + 1) * POOL_GROUP_DIM)
        acc = h[:, cols]
        for k in range(1, w):
            acc = acc + ext_ref[POOL_HALO - k:POOL_HALO - k + t, cols]
        cnt = jnp.minimum(row + 1, w).astype(F32)
        y = acc / cnt - h[:, cols]
        ys.append(jnp.dot(y.astype(BF16), w_ref[g], preferred_element_type=F32))
    y = jnp.concatenate(ys, axis=-1) * ps_ref[...]
    z = DEEPNORM_ALPHA * x + gate_ref[0] * y
    o_ref[0] = _layer_norm(z, g_ref[...], b_ref[...])


def _pool_layer(x, shift, scale, gate, w_pool, pool_scale, ln_g, ln_b):
    b, s, d = x.shape
    t = min(s, 512)
    halo_blocks = t // POOL_HALO
    mod_spec = pl.BlockSpec((1, 1, d), lambda bi, i: (bi, 0, 0))
    return pl.pallas_call(
        _pool_kernel,
        out_shape=jax.ShapeDtypeStruct((b, s, d), F32),
        grid=(b, s // t),
        in_specs=[
            pl.BlockSpec((1, t, d), lambda bi, i: (bi, i, 0)),
            pl.BlockSpec((1, POOL_HALO, d),
                         lambda bi, i: (bi, jnp.maximum(i * halo_blocks - 1, 0), 0)),
            mod_spec, mod_spec, mod_spec,
            _const_spec((N_POOL_GROUPS, POOL_GROUP_DIM, POOL_GROUP_DIM)),
            _const_spec((1, d)), _const_spec((1, d)), _const_spec((1, d)),
        ],
        out_specs=pl.BlockSpec((1, t, d), lambda bi, i: (bi, i, 0)),
        scratch_shapes=[pltpu.VMEM((POOL_HALO + t, d), F32)],
        compiler_params=_params(("arbitrary", "arbitrary")),
        name="pool_layer",
    )(x, x, shift, scale, gate, w_pool, pool_scale, ln_g, ln_b)


def _mlp_kernel(x_ref, shift_ref, scale_ref, gate_ref, w1_ref, w2_ref, g_ref, b_ref, o_ref,
                *, ff_chunk, sub_rows):
    n_sub = x_ref.shape[1] // sub_rows
    n_chunks = D_FF // ff_chunk
    mul = 1.0 + scale_ref[0]
    shift = shift_ref[0]

    def hidden(r):
        rows = slice(r * sub_rows, (r + 1) * sub_rows)
        return (x_ref[0, rows, :] * mul + shift).astype(BF16)

    def ff_part(h, c):
        cols = slice(c * ff_chunk, (c + 1) * ff_chunk)
        a = jnp.maximum(jnp.dot(h, w1_ref[:, cols], preferred_element_type=F32), 0.0)
        return jnp.dot((a * a).astype(BF16), w2_ref[cols, :], preferred_element_type=F32)

    def finish(r, y):
        rows = slice(r * sub_rows, (r + 1) * sub_rows)
        z = DEEPNORM_ALPHA * x_ref[0, rows, :] + gate_ref[0] * y
        o_ref[0, rows, :] = _layer_norm(z, g_ref[...], b_ref[...])

    prev = None
    for r in range(n_sub):
        h = hidden(r)
        y = ff_part(h, 0)
        if prev is not None:
            finish(r - 1, prev)
        for c in range(1, n_chunks):
            y = y + ff_part(h, c)
        prev = y
    finish(n_sub - 1, prev)


def _mlp_layer(x, shift, scale, gate, w1, w2, ln_g, ln_b):
    b, s, d = x.shape
    t = min(s, 1024)
    mod_spec = pl.BlockSpec((1, 1, d), lambda bi, i: (bi, 0, 0))
    return pl.pallas_call(
        functools.partial(_mlp_kernel, ff_chunk=1024, sub_rows=min(t, 512)),
        out_shape=jax.ShapeDtypeStruct((b, s, d), F32),
        grid=(b, s // t),
        in_specs=[
            pl.BlockSpec((1, t, d), lambda bi, i: (bi, i, 0)),
            mod_spec, mod_spec, mod_spec,
            _const_spec((d, D_FF)), _const_spec((D_FF, d)),
            _const_spec((1, d)), _const_spec((1, d)),
        ],
        out_specs=pl.BlockSpec((1, t, d), lambda bi, i: (bi, i, 0)),
        compiler_params=_params(("arbitrary", "arbitrary")),
        name="mlp_layer",
    )(x, shift, scale, gate, w1, w2, ln_g, ln_b)


def _kv_kernel(x_ref, tab_ref, wc_ref, wr_ref, ng_ref, kup_ref, vup_ref, k_ref, vt_ref):
    xb = x_ref[0].astype(BF16)
    c = jnp.dot(xb, wc_ref[...], preferred_element_type=F32)
    ms = jnp.mean(c * c, axis=-1, keepdims=True)
    ckv = (c * lax.rsqrt(ms + RMS_EPS) * ng_ref[...]).astype(BF16)
    y = jnp.dot(xb, wr_ref[...], preferred_element_type=F32) * tab_ref[0]
    lane = lax.broadcasted_iota(jnp.int32, y.shape, 1)
    kr = jnp.where(lane < QK_ROPE_DIM, _rope_pair_sum(y), 0.0).astype(BF16)
    kn = jnp.dot(ckv, kup_ref[...], preferred_element_type=F32).astype(BF16)
    for h in range(N_HEADS):
        k_ref[0, :, h * HEAD_W:h * HEAD_W + LANES] = kn[:, h * LANES:(h + 1) * LANES]
        k_ref[0, :, h * HEAD_W + LANES:(h + 1) * HEAD_W] = kr
    vt_ref[0] = lax.dot_general(vup_ref[...], ckv, _NT_DIMS,
                                preferred_element_type=F32).astype(BF16)


def _kv_proj(x, tab, w_c, w_r2, norm_g, k_up, v_up_t):
    b, s, d = x.shape
    t = min(s, 512)
    return pl.pallas_call(
        _kv_kernel,
        out_shape=(jax.ShapeDtypeStruct((b, s, N_HEADS * HEAD_W), BF16),
                   jax.ShapeDtypeStruct((b, N_HEADS * V_HEAD_DIM, s), BF16)),
        grid=(b, s // t),
        in_specs=[
            pl.BlockSpec((1, t, d), lambda bi, i: (bi, i, 0)),
            pl.BlockSpec((1, t, LANES), lambda bi, i: (bi, i, 0)),
            _const_spec((d, KV_LORA_RANK)), _const_spec((d, LANES)),
            _const_spec((1, KV_LORA_RANK)),
            _const_spec((KV_LORA_RANK, N_HEADS * QK_NOPE_DIM)),
            _const_spec((N_HEADS * V_HEAD_DIM, KV_LORA_RANK)),
        ],
        out_specs=(pl.BlockSpec((1, t, N_HEADS * HEAD_W), lambda bi, i: (bi, i, 0)),
                   pl.BlockSpec((1, N_HEADS * V_HEAD_DIM, t), lambda bi, i: (bi, 0, i))),
        compiler_params=_params(("arbitrary", "arbitrary")),
        name="kv_proj",
    )(x, tab, w_c, w_r2, norm_g, k_up, v_up_t)


def _q_kernel(x_ref, tabt_ref, shift_ref, scale_ref, wd_ref, ng_ref, wn_ref, wr_ref, qt_ref):
    h = (x_ref[0] * (1.0 + scale_ref[0]) + shift_ref[0]).astype(BF16)
    c = jnp.dot(h, wd_ref[...], preferred_element_type=F32)
    ms = jnp.mean(c * c, axis=-1, keepdims=True)
    cq = (c * lax.rsqrt(ms + RMS_EPS) * ng_ref[...]).astype(BF16)
    qn = lax.dot_general(wn_ref[...], cq, _NT_DIMS, preferred_element_type=F32)
    qr = lax.dot_general(wr_ref[...], cq, _NT_DIMS, preferred_element_type=F32)
    tabt = tabt_ref[0]
    for hd in range(N_HEADS):
        rows = slice(hd * LANES, (hd + 1) * LANES)
        qt_ref[0, hd * HEAD_W:hd * HEAD_W + LANES, :] = (qn[rows] * Q_SCALE).astype(BF16)
        y = qr[rows] * tabt
        rot = ((y[:QK_ROPE_DIM] + y[QK_ROPE_DIM:]) * Q_SCALE).astype(BF16)
        qt_ref[0, hd * HEAD_W + LANES:hd * HEAD_W + LANES + QK_ROPE_DIM, :] = rot
        qt_ref[0, hd * HEAD_W + LANES + QK_ROPE_DIM:(hd + 1) * HEAD_W, :] = rot


def _q_proj(x, tabt, shift, scale, w_down, norm_g, w_nope_t, w_rope2_t):
    b, s, d = x.shape
    t = min(s, 512)
    mod_spec = pl.BlockSpec((1, 1, d), lambda bi, i: (bi, 0, 0))
    return pl.pallas_call(
        _q_kernel,
        out_shape=jax.ShapeDtypeStruct((b, N_HEADS * HEAD_W, s), BF16),
        grid=(b, s // t),
        in_specs=[
            pl.BlockSpec((1, t, d), lambda bi, i: (bi, i, 0)),
            pl.BlockSpec((1, LANES, t), lambda bi, i: (bi, 0, i)),
            mod_spec, mod_spec,
            _const_spec((d, Q_LORA_RANK)), _const_spec((1, Q_LORA_RANK)),
            _const_spec((N_HEADS * LANES, Q_LORA_RANK)),
            _const_spec((N_HEADS * LANES, Q_LORA_RANK)),
        ],
        out_specs=pl.BlockSpec((1, N_HEADS * HEAD_W, t), lambda bi, i: (bi, 0, i)),
        compiler_params=_params(("arbitrary", "arbitrary")),
        name="q_proj",
    )(x, tabt, shift, scale, w_down, norm_g, w_nope_t, w_rope2_t)


def _attn_kernel(qi_ref, kj_ref, qt_ref, k_ref, vt_ref, x_ref, gate_ref, wo_ref, g_ref, b_ref,
                 o_ref, m_ref, l_ref, acc_ref):
    p = pl.program_id(1)
    i = qi_ref[p]
    j = kj_ref[p]
    tq = qt_ref.shape[2]
    tk = k_ref.shape[1]

    @pl.when(j == 0)
    def _():
        m_ref[...] = jnp.full(m_ref.shape, -jnp.inf, F32)
        l_ref[...] = jnp.zeros(l_ref.shape, F32)
        acc_ref[...] = jnp.zeros(acc_ref.shape, F32)

    def step(diagonal):
        if diagonal:
            keep = (lax.broadcasted_iota(jnp.int32, (tk, tq), 0)
                    <= lax.broadcasted_iota(jnp.int32, (tk, tq), 1))

        n_strips = tq // STRIP
        n_items = N_HEADS * n_strips

        def scores(u):
            h, c = divmod(u, n_strips)
            cols = slice(c * STRIP, (c + 1) * STRIP)
            st = jnp.dot(k_ref[0, :, h * HEAD_W:(h + 1) * HEAD_W],
                         qt_ref[0, h * HEAD_W:(h + 1) * HEAD_W, cols],
                         preferred_element_type=F32)
            return jnp.where(keep[:, cols], st, -jnp.inf) if diagonal else st

        ones_rows = jnp.ones((BF16_SUBLANES, tk), BF16)

        def softmax(u, st):
            h, c = divmod(u, n_strips)
            cols = slice(c * STRIP, (c + 1) * STRIP)
            m_prev = m_ref[h, :, cols]
            m_new = jnp.maximum(m_prev, jnp.max(st, axis=0, keepdims=True))
            m_ref[h, :, cols] = m_new
            return jnp.exp2(m_prev - m_new), jnp.exp2((st - m_new).astype(BF16))

        def accumulate(u, alpha, e):
            h, c = divmod(u, n_strips)
            cols = slice(c * STRIP, (c + 1) * STRIP)
            rows = slice(h * V_HEAD_DIM, (h + 1) * V_HEAD_DIM)
            vt_ones = jnp.concatenate([vt_ref[0, rows, :], ones_rows], axis=0)
            pv = jnp.dot(vt_ones, e, preferred_element_type=F32)
            acc_ref[rows, cols] = alpha * acc_ref[rows, cols] + pv[:V_HEAD_DIM]
            l_ref[h, :, cols] = alpha * l_ref[h, :, cols] + pv[V_HEAD_DIM:V_HEAD_DIM + 1]

        st = {0: scores(0), 1: scores(1)}
        pe = {0: softmax(0, st.pop(0))}
        for u in range(n_items):
            accumulate(u, *pe.pop(u))
            if u + 2 < n_items:
                st[u + 2] = scores(u + 2)
            if u + 1 < n_items:
                pe[u + 1] = softmax(u + 1, st.pop(u + 1))

    @pl.when(j < i)
    def _():
        step(False)

    @pl.when(j == i)
    def _():
        step(True)
        o = jnp.concatenate(
            [(acc_ref[h * V_HEAD_DIM:(h + 1) * V_HEAD_DIM, :] / l_ref[h]).T.astype(BF16)
             for h in range(N_HEADS)], axis=-1)
        y = jnp.dot(o, wo_ref[...], preferred_element_type=F32)
        z = DEEPNORM_ALPHA * x_ref[0] + gate_ref[0] * y
        o_ref[0] = _layer_norm(z, g_ref[...], b_ref[...])


def _attn_layer(x, qt, k, vt, gate, w_out, ln_g, ln_b):
    b, s, d = x.shape
    t = min(s, 512)
    nq = s // t
    pairs = [(i, j) for i in range(nq) for j in range(i + 1)]
    qi = jnp.asarray(np.array([p[0] for p in pairs], np.int32))
    kj = jnp.asarray(np.array([p[1] for p in pairs], np.int32))
    grid_spec = pltpu.PrefetchScalarGridSpec(
        num_scalar_prefetch=2,
        grid=(b, len(pairs)),
        in_specs=[
            pl.BlockSpec((1, N_HEADS * HEAD_W, t), lambda bi, p, qi, kj: (bi, 0, qi[p])),
            pl.BlockSpec((1, t, N_HEADS * HEAD_W), lambda bi, p, qi, kj: (bi, kj[p], 0)),
            pl.BlockSpec((1, N_HEADS * V_HEAD_DIM, t), lambda bi, p, qi, kj: (bi, 0, kj[p])),
            pl.BlockSpec((1, t, d), lambda bi, p, qi, kj: (bi, qi[p], 0)),
            pl.BlockSpec((1, 1, d), lambda bi, p, qi, kj: (bi, 0, 0)),
            _const_spec((d, d)), _const_spec((1, d)), _const_spec((1, d)),
        ],
        out_specs=pl.BlockSpec((1, t, d), lambda bi, p, qi, kj: (bi, qi[p], 0)),
        scratch_shapes=[
            pltpu.VMEM((N_HEADS, 1, t), F32),
            pltpu.VMEM((N_HEADS, 1, t), F32),
            pltpu.VMEM((N_HEADS * V_HEAD_DIM, t), F32),
        ],
    )
    return pl.pallas_call(
        _attn_kernel,
        out_shape=jax.ShapeDtypeStruct((b, s, d), F32),
        grid_spec=grid_spec,
        compiler_params=_params(("arbitrary", "arbitrary")),
        name="attn_layer",
    )(qi, kj, qt, k, vt, x, gate, w_out, ln_g, ln_b)


def _swap_halves(w):
    half = w.shape[-1] // 2
    return jnp.concatenate([w[..., half:], w[..., :half]], axis=-1)


def _split_q_up(w):
    r = w.shape[0]
    w = w.reshape(r, N_HEADS, QK_NOPE_DIM + QK_ROPE_DIM)
    nope = w[..., :QK_NOPE_DIM].reshape(r, N_HEADS * QK_NOPE_DIM)
    rope = w[..., QK_NOPE_DIM:]
    rope2 = jnp.concatenate([rope, _swap_halves(rope)], axis=-1).reshape(r, N_HEADS * LANES)
    return nope.T.astype(BF16), rope2.T.astype(BF16)


def kernel(x, c, positions, ada_w, ada_b, ln_g, ln_b, mlp_w1, mlp_w2, pool_w, pool_scale,
           q_down_w, q_norm_g, q_up_w, attn_out_w, kv_in_w, kv_norm_g, k_up_w, v_up_w):
    b, s, d = x.shape
    mod = _adaln_mod(c, ada_w, ada_b).reshape(DEPTH, b, N_MOD, 1, d)
    tab, tabt = _rope_table(positions)

    kv_rope_w = kv_in_w[:, KV_LORA_RANK:]
    kv_w_c = kv_in_w[:, :KV_LORA_RANK].astype(BF16)
    kv_w_r2 = jnp.concatenate([kv_rope_w, _swap_halves(kv_rope_w)], axis=-1).astype(BF16)

    k = vt = None
    for l in range(DEPTH):
        shift1, scale1, gate1, shift2, scale2, gate2 = (mod[l, :, m] for m in range(N_MOD))
        g1, b1 = ln_g[l, 0].reshape(1, d), ln_b[l, 0].reshape(1, d)
        g2, b2 = ln_g[l, 1].reshape(1, d), ln_b[l, 1].reshape(1, d)
        if l < N_A_LAYERS:
            x = _pool_layer(x, shift1, scale1, gate1, pool_w[l].astype(BF16),
                            pool_scale[l].reshape(1, d), g1, b1)
        else:
            if l == N_A_LAYERS:
                k, vt = _kv_proj(x, tab, kv_w_c, kv_w_r2, kv_norm_g.reshape(1, KV_LORA_RANK),
                                 k_up_w.astype(BF16), v_up_w.T.astype(BF16))
            jl = l - N_A_LAYERS
            w_nope_t, w_rope2_t = _split_q_up(q_up_w[jl])
            qt = _q_proj(x, tabt, shift1, scale1, q_down_w[jl].astype(BF16),
                         q_norm_g[jl].reshape(1, Q_LORA_RANK), w_nope_t, w_rope2_t)
            x = _attn_layer(x, qt, k, vt, gate1, attn_out_w[jl].astype(BF16), g1, b1)
        x = _mlp_layer(x, shift2, scale2, gate2, mlp_w1[l].astype(BF16), mlp_w2[l].astype(BF16),
                       g2, b2)
    return x
```

```python
import functools

import numpy as np
import jax
import jax.numpy as jnp
from jax import lax
from jax.experimental import pallas as pl
from jax.experimental.pallas import tpu as pltpu

D_MODEL = 1024
DEPTH = 4
N_A_LAYERS = DEPTH // 2
POOL_WINDOWS = (2, 4, 8, 16)
N_POOL_GROUPS = len(POOL_WINDOWS)
POOL_GROUP_DIM = D_MODEL // N_POOL_GROUPS
POOL_HALO = 16
QK_NOPE_DIM = 128
QK_ROPE_DIM = 64
V_HEAD_DIM = 128
N_HEADS = D_MODEL // 128
Q_LORA_RANK = D_MODEL // 2
KV_LORA_RANK = D_MODEL // 4
ROPE_THETA = 10000.0
D_FF = 4 * D_MODEL
N_MOD = 6
DEEPNORM_ALPHA = (2.0 * DEPTH) ** 0.25
LN_EPS = 1e-5
RMS_EPS = 1e-6
ATTN_SCALE = (QK_NOPE_DIM + QK_ROPE_DIM) ** -0.5
Q_SCALE = ATTN_SCALE * float(np.log2(np.e))

LANES = 128
BF16_SUBLANES = 16
ATTN_TILE = 512
SCORE_BOUND = 60.0
HEAD_W = 2 * LANES
ROPE_HALF = QK_ROPE_DIM // 2

VMEM_LIMIT = 56 * 1024 * 1024

F32 = jnp.float32
BF16 = jnp.bfloat16

_NT_DIMS = (((1,), (1,)), ((), ()))


def _params(semantics):
    return pltpu.CompilerParams(dimension_semantics=semantics, vmem_limit_bytes=VMEM_LIMIT)


def _const_spec(shape):
    nd = len(shape)
    return pl.BlockSpec(shape, lambda *_: (0,) * nd, pipeline_mode=pl.Buffered(1))


def _layer_norm(z, g, b):
    mu = jnp.mean(z, axis=-1, keepdims=True)
    zc = z - mu
    var = jnp.mean(zc * zc, axis=-1, keepdims=True)
    return zc * lax.rsqrt(var + LN_EPS) * g + b


def _rope_pair_sum(y):
    return y + pltpu.roll(y, QK_ROPE_DIM, axis=1)


def _adaln_kernel(c_ref, w_ref, b_ref, o_ref):
    c = c_ref[...]
    sc = c * jax.nn.sigmoid(c)
    o_ref[0] = jnp.dot(sc.astype(BF16), w_ref[0].astype(BF16),
                       preferred_element_type=F32) + b_ref[0]


def _adaln_mod(c, ada_w, ada_b):
    b, d = c.shape
    n = ada_w.shape[-1]
    tn = 1536
    return pl.pallas_call(
        _adaln_kernel,
        out_shape=jax.ShapeDtypeStruct((DEPTH, b, n), F32),
        grid=(DEPTH, n // tn),
        in_specs=[
            pl.BlockSpec((b, d), lambda l, j: (0, 0)),
            pl.BlockSpec((1, d, tn), lambda l, j: (l, 0, j)),
            pl.BlockSpec((1, 1, tn), lambda l, j: (l, 0, j)),
        ],
        out_specs=pl.BlockSpec((1, b, tn), lambda l, j: (l, 0, j)),
        compiler_params=_params(("arbitrary", "arbitrary")),
        name="adaln_mod",
    )(c, ada_w, ada_b.reshape(DEPTH, 1, n))


def _rope_table_kernel(pos_ref, freq_ref, o_ref, ot_ref):
    pos = pos_ref[0].astype(F32)
    ang = freq_ref[...] * pos
    cos = jnp.cos(ang)
    sin = jnp.sin(ang)
    tab = jnp.concatenate([cos, cos, -sin, sin], axis=0)
    ot_ref[0] = tab
    o_ref[0] = tab.T


def _rope_table(positions):
    b, s = positions.shape
    t = min(s, 1024)
    inv_freq = ROPE_THETA ** (-jnp.arange(0, QK_ROPE_DIM, 2, dtype=F32) / QK_ROPE_DIM)
    return pl.pallas_call(
        _rope_table_kernel,
        out_shape=(jax.ShapeDtypeStruct((b, s, LANES), F32),
                   jax.ShapeDtypeStruct((b, LANES, s), F32)),
        grid=(b, s // t),
        in_specs=[
            pl.BlockSpec((1, 1, t), lambda bi, i: (bi, 0, i)),
            pl.BlockSpec((ROPE_HALF, 1), lambda bi, i: (0, 0)),
        ],
        out_specs=(pl.BlockSpec((1, t, LANES), lambda bi, i: (bi, i, 0)),
                   pl.BlockSpec((1, LANES, t), lambda bi, i: (bi, 0, i))),
        compiler_params=_params(("arbitrary", "arbitrary")),
        name="rope_table",
    )(positions.reshape(b, 1, s), inv_freq.reshape(ROPE_HALF, 1))


def _pool_kernel(x_ref, xh_ref, shift_ref, scale_ref, gate_ref, w_ref, ps_ref, g_ref, b_ref,
                 o_ref, ext_ref):
    i = pl.program_id(1)
    t = x_ref.shape[1]
    x = x_ref[0]
    mul = 1.0 + scale_ref[0]
    shift = shift_ref[0]
    h = x * mul + shift
    hh = xh_ref[0] * mul + shift
    ext_ref[0:POOL_HALO, :] = jnp.where(i == 0, 0.0, hh)
    ext_ref[POOL_HALO:, :] = h

    row = i * t + lax.broadcasted_iota(jnp.int32, (t, 1), 0)
    ys = []
    for g, w in enumerate(POOL_WINDOWS):
        cols = slice(g * POOL_GROUP_DIM, (g + 1) * POOL_GROUP_DIM)
        acc = h[:, cols]
        for k in range(1, w):
            acc = acc + ext_ref[POOL_HALO - k:POOL_HALO - k + t, cols]
        cnt = jnp.minimum(row + 1, w).astype(F32)
        y = acc / cnt - h[:, cols]
        ys.append(jnp.dot(y.astype(BF16), w_ref[g], preferred_element_type=F32))
    y = jnp.concatenate(ys, axis=-1) * ps_ref[...]
    z = DEEPNORM_ALPHA * x + gate_ref[0] * y
    o_ref[0] = _layer_norm(z, g_ref[...], b_ref[...])


def _pool_layer(x, shift, scale, gate, w_pool, pool_scale, ln_g, ln_b):
    b, s, d = x.shape
    t = min(s, 512)
    halo_blocks = t // POOL_HALO
    mod_spec = pl.BlockSpec((1, 1, d), lambda bi, i: (bi, 0, 0))
    return pl.pallas_call(
        _pool_kernel,
        out_shape=jax.ShapeDtypeStruct((b, s, d), F32),
        grid=(b, s // t),
        in_specs=[
            pl.BlockSpec((1, t, d), lambda bi, i: (bi, i, 0)),
            pl.BlockSpec((1, POOL_HALO, d),
                         lambda bi, i: (bi, jnp.maximum(i * halo_blocks - 1, 0), 0)),
            mod_spec, mod_spec, mod_spec,
            _const_spec((N_POOL_GROUPS, POOL_GROUP_DIM, POOL_GROUP_DIM)),
            _const_spec((1, d)), _const_spec((1, d)), _const_spec((1, d)),
        ],
        out_specs=pl.BlockSpec((1, t, d), lambda bi, i: (bi, i, 0)),
        scratch_shapes=[pltpu.VMEM((POOL_HALO + t, d), F32)],
        compiler_params=_params(("arbitrary", "arbitrary")),
        name="pool_layer",
    )(x, x, shift, scale, gate, w_pool, pool_scale, ln_g, ln_b)


def _mlp_kernel(x_ref, shift_ref, scale_ref, gate_ref, w1_ref, w2_ref, g_ref, b_ref, o_ref,
                *, ff_chunk, sub_rows):
    n_sub = x_ref.shape[1] // sub_rows
    n_chunks = D_FF // ff_chunk
    mul = 1.0 + scale_ref[0]
    shift = shift_ref[0]

    def hidden(r):
        rows = slice(r * sub_rows, (r + 1) * sub_rows)
        return (x_ref[0, rows, :] * mul + shift).astype(BF16)

    def ff_part(h, c):
        cols = slice(c * ff_chunk, (c + 1) * ff_chunk)
        a = jnp.maximum(jnp.dot(h, w1_ref[:, cols], preferred_element_type=F32), 0.0)
        return jnp.dot((a * a).astype(BF16), w2_ref[cols, :], preferred_element_type=F32)

    def finish(r, y):
        rows = slice(r * sub_rows, (r + 1) * sub_rows)
        z = DEEPNORM_ALPHA * x_ref[0, rows, :] + gate_ref[0] * y
        o_ref[0, rows, :] = _layer_norm(z, g_ref[...], b_ref[...])

    prev = None
    for r in range(n_sub):
        h = hidden(r)
        y = ff_part(h, 0)
        if prev is not None:
            finish(r - 1, prev)
        for c in range(1, n_chunks):
            y = y + ff_part(h, c)
        prev = y
    finish(n_sub - 1, prev)


def _mlp_layer(x, shift, scale, gate, w1, w2, ln_g, ln_b):
    b, s, d = x.shape
    t = min(s, 1024)
    mod_spec = pl.BlockSpec((1, 1, d), lambda bi, i: (bi, 0, 0))
    return pl.pallas_call(
        functools.partial(_mlp_kernel, ff_chunk=1024, sub_rows=min(t, 512)),
        out_shape=jax.ShapeDtypeStruct((b, s, d), F32),
        grid=(b, s // t),
        in_specs=[
            pl.BlockSpec((1, t, d), lambda bi, i: (bi, i, 0)),
            mod_spec, mod_spec, mod_spec,
            _const_spec((d, D_FF)), _const_spec((D_FF, d)),
            _const_spec((1, d)), _const_spec((1, d)),
        ],
        out_specs=pl.BlockSpec((1, t, d), lambda bi, i: (bi, i, 0)),
        compiler_params=_params(("arbitrary", "arbitrary")),
        name="mlp_layer",
    )(x, shift, scale, gate, w1, w2, ln_g, ln_b)


def _kv_kernel(x_ref, tab_ref, wc_ref, wr_ref, ng_ref, kup_ref, vup_ref, k_ref, vt_ref, n2_ref):
    xb = x_ref[0].astype(BF16)
    c = jnp.dot(xb, wc_ref[...], preferred_element_type=F32)
    ms = jnp.mean(c * c, axis=-1, keepdims=True)
    ckv = (c * lax.rsqrt(ms + RMS_EPS) * ng_ref[...]).astype(BF16)
    y = jnp.dot(xb, wr_ref[...], preferred_element_type=F32) * tab_ref[0]
    lane = lax.broadcasted_iota(jnp.int32, y.shape, 1)
    kr = jnp.where(lane < QK_ROPE_DIM, _rope_pair_sum(y), 0.0)
    kn = jnp.dot(ckv, kup_ref[...], preferred_element_type=F32)
    kr_n2 = jnp.sum(kr * kr, axis=1, keepdims=True)
    n2 = []
    for h in range(N_HEADS):
        kn_h = kn[:, h * LANES:(h + 1) * LANES]
        k_ref[0, :, h * HEAD_W:h * HEAD_W + LANES] = kn_h.astype(BF16)
        k_ref[0, :, h * HEAD_W + LANES:(h + 1) * HEAD_W] = kr.astype(BF16)
        key_n2 = jnp.sum(kn_h * kn_h, axis=1, keepdims=True) + kr_n2
        n2.append(jnp.broadcast_to(jnp.max(key_n2, axis=0, keepdims=True), (1, LANES)))
    n2_ref[0, 0] = jnp.concatenate(n2, axis=0)
    vt_ref[0] = lax.dot_general(vup_ref[...], ckv, _NT_DIMS,
                                preferred_element_type=F32).astype(BF16)


def _kv_proj(x, tab, w_c, w_r2, norm_g, k_up, v_up_t):
    b, s, d = x.shape
    t = min(s, ATTN_TILE)
    return pl.pallas_call(
        _kv_kernel,
        out_shape=(jax.ShapeDtypeStruct((b, s, N_HEADS * HEAD_W), BF16),
                   jax.ShapeDtypeStruct((b, N_HEADS * V_HEAD_DIM, s), BF16),
                   jax.ShapeDtypeStruct((b, s // t, N_HEADS, LANES), F32)),
        grid=(b, s // t),
        in_specs=[
            pl.BlockSpec((1, t, d), lambda bi, i: (bi, i, 0)),
            pl.BlockSpec((1, t, LANES), lambda bi, i: (bi, i, 0)),
            _const_spec((d, KV_LORA_RANK)), _const_spec((d, LANES)),
            _const_spec((1, KV_LORA_RANK)),
            _const_spec((KV_LORA_RANK, N_HEADS * QK_NOPE_DIM)),
            _const_spec((N_HEADS * V_HEAD_DIM, KV_LORA_RANK)),
        ],
        out_specs=(pl.BlockSpec((1, t, N_HEADS * HEAD_W), lambda bi, i: (bi, i, 0)),
                   pl.BlockSpec((1, N_HEADS * V_HEAD_DIM, t), lambda bi, i: (bi, 0, i)),
                   pl.BlockSpec((1, 1, N_HEADS, LANES), lambda bi, i: (bi, i, 0, 0))),
        compiler_params=_params(("arbitrary", "arbitrary")),
        name="kv_proj",
    )(x, tab, w_c, w_r2, norm_g, k_up, v_up_t)


def _q_kernel(x_ref, tabt_ref, shift_ref, scale_ref, wd_ref, ng_ref, wn_ref, wr_ref, qt_ref,
              n2_ref):
    h = (x_ref[0] * (1.0 + scale_ref[0]) + shift_ref[0]).astype(BF16)
    c = jnp.dot(h, wd_ref[...], preferred_element_type=F32)
    ms = jnp.mean(c * c, axis=-1, keepdims=True)
    cq = (c * lax.rsqrt(ms + RMS_EPS) * ng_ref[...]).astype(BF16)
    qn = lax.dot_general(wn_ref[...], cq, _NT_DIMS, preferred_element_type=F32)
    qr = lax.dot_general(wr_ref[...], cq, _NT_DIMS, preferred_element_type=F32)
    tabt = tabt_ref[0]
    t = tabt.shape[1]
    n2 = []
    for hd in range(N_HEADS):
        rows = slice(hd * LANES, (hd + 1) * LANES)
        nope = qn[rows] * Q_SCALE
        qt_ref[0, hd * HEAD_W:hd * HEAD_W + LANES, :] = nope.astype(BF16)
        y = qr[rows] * tabt
        rot = (y[:QK_ROPE_DIM] + y[QK_ROPE_DIM:]) * Q_SCALE
        qt_ref[0, hd * HEAD_W + LANES:hd * HEAD_W + LANES + QK_ROPE_DIM, :] = rot.astype(BF16)
        qt_ref[0, hd * HEAD_W + LANES + QK_ROPE_DIM:(hd + 1) * HEAD_W, :] = rot.astype(BF16)
        query_n2 = (jnp.sum(nope * nope, axis=0, keepdims=True)
                    + jnp.sum(rot * rot, axis=0, keepdims=True))
        folded = query_n2[:, :LANES]
        for c in range(1, t // LANES):
            folded = jnp.maximum(folded, query_n2[:, c * LANES:(c + 1) * LANES])
        n2.append(folded)
    n2_ref[0, 0] = jnp.concatenate(n2, axis=0)


def _q_proj(x, tabt, shift, scale, w_down, norm_g, w_nope_t, w_rope2_t):
    b, s, d = x.shape
    t = min(s, ATTN_TILE)
    mod_spec = pl.BlockSpec((1, 1, d), lambda bi, i: (bi, 0, 0))
    return pl.pallas_call(
        _q_kernel,
        out_shape=(jax.ShapeDtypeStruct((b, N_HEADS * HEAD_W, s), BF16),
                   jax.ShapeDtypeStruct((b, s // t, N_HEADS, LANES), F32)),
        grid=(b, s // t),
        in_specs=[
            pl.BlockSpec((1, t, d), lambda bi, i: (bi, i, 0)),
            pl.BlockSpec((1, LANES, t), lambda bi, i: (bi, 0, i)),
            mod_spec, mod_spec,
            _const_spec((d, Q_LORA_RANK)), _const_spec((1, Q_LORA_RANK)),
            _const_spec((N_HEADS * LANES, Q_LORA_RANK)),
            _const_spec((N_HEADS * LANES, Q_LORA_RANK)),
        ],
        out_specs=(pl.BlockSpec((1, N_HEADS * HEAD_W, t), lambda bi, i: (bi, 0, i)),
                   pl.BlockSpec((1, 1, N_HEADS, LANES), lambda bi, i: (bi, i, 0, 0))),
        compiler_params=_params(("arbitrary", "arbitrary")),
        name="q_proj",
    )(x, tabt, shift, scale, w_down, norm_g, w_nope_t, w_rope2_t)


def _attn_kernel(qi_ref, kj_ref, bounded_ref, qt_ref, k_ref, vt_ref, x_ref, gate_ref, wo_ref,
                 g_ref, b_ref, o_ref, m_ref, l_ref, acc_ref):
    p = pl.program_id(1)
    i = qi_ref[p]
    j = kj_ref[p]
    bounded = bounded_ref[pl.program_id(0) * pl.num_programs(1) + p] != 0
    tk = k_ref.shape[1]

    @pl.when(j == 0)
    def _():
        m_ref[...] = jnp.full(m_ref.shape, -jnp.inf, F32)
        l_ref[...] = jnp.zeros(l_ref.shape, F32)
        acc_ref[...] = jnp.zeros(acc_ref.shape, F32)

    half = tk // 2
    ones_rows = jnp.ones((BF16_SUBLANES, tk), BF16)

    def head_operands(h):
        return (k_ref[0, :, h * HEAD_W:(h + 1) * HEAD_W],
                qt_ref[0, h * HEAD_W:(h + 1) * HEAD_W, :])

    def scores_full(h):
        k, qt = head_operands(h)
        return (jnp.dot(k, qt, preferred_element_type=F32),)

    def softmax_full(h, st):
        m_prev = m_ref[h]
        m_new = jnp.maximum(m_prev, jnp.max(st, axis=0, keepdims=True))
        m_ref[h] = m_new
        return jnp.exp2(m_prev - m_new), jnp.exp2((st - m_new).astype(BF16))

    def softmax_bounded(h, *sts):
        return (None,) + tuple(jnp.exp2(st).astype(BF16) for st in sts)

    def values_full(h, vt_ones, e):
        return jnp.dot(vt_ones, e, preferred_element_type=F32)

    def scores_diag(h):
        k, qt = head_operands(h)
        tri = (lax.broadcasted_iota(jnp.int32, (half, half), 0)
               <= lax.broadcasted_iota(jnp.int32, (half, half), 1))
        st_lo = jnp.dot(k[:half], qt, preferred_element_type=F32)
        st_hi = jnp.dot(k[half:], qt[:, half:], preferred_element_type=F32)
        st_lo = jnp.concatenate([jnp.where(tri, st_lo[:, :half], -jnp.inf), st_lo[:, half:]],
                                axis=1)
        return st_lo, jnp.where(tri, st_hi, -jnp.inf)

    def softmax_diag(h, st_lo, st_hi):
        m_prev = m_ref[h]
        mx_lo = jnp.max(st_lo, axis=0, keepdims=True)
        mx_hi = jnp.max(st_hi, axis=0, keepdims=True)
        m_blk = jnp.concatenate([mx_lo[:, :half], jnp.maximum(mx_lo[:, half:], mx_hi)], axis=1)
        m_new = jnp.maximum(m_prev, m_blk)
        m_ref[h] = m_new
        return (jnp.exp2(m_prev - m_new), jnp.exp2((st_lo - m_new).astype(BF16)),
                jnp.exp2((st_hi - m_new[:, half:]).astype(BF16)))

    def values_diag(h, vt_ones, e_lo, e_hi):
        pv_lo = jnp.dot(vt_ones[:, :half], e_lo, preferred_element_type=F32)
        pv_hi = jnp.dot(vt_ones[:, half:], e_hi, preferred_element_type=F32)
        return jnp.concatenate([pv_lo[:, :half], pv_lo[:, half:] + pv_hi], axis=1)

    def step(scores, softmax, values):
        def accumulate(h, alpha, *e):
            rows = slice(h * V_HEAD_DIM, (h + 1) * V_HEAD_DIM)
            vt_ones = jnp.concatenate([vt_ref[0, rows, :], ones_rows], axis=0)
            pv = values(h, vt_ones, *e)
            if alpha is None:
                acc_ref[rows, :] += pv[:V_HEAD_DIM]
                l_ref[h] += pv[V_HEAD_DIM:V_HEAD_DIM + 1]
            else:
                acc_ref[rows, :] = alpha * acc_ref[rows, :] + pv[:V_HEAD_DIM]
                l_ref[h] = alpha * l_ref[h] + pv[V_HEAD_DIM:V_HEAD_DIM + 1]

        st = {0: scores(0), 1: scores(1)}
        pe = {0: softmax(0, *st.pop(0))}
        for h in range(N_HEADS):
            accumulate(h, *pe.pop(h))
            if h + 2 < N_HEADS:
                st[h + 2] = scores(h + 2)
            if h + 1 < N_HEADS:
                pe[h + 1] = softmax(h + 1, *st.pop(h + 1))

    @pl.when(jnp.logical_and(j < i, bounded))
    def _():
        step(scores_full, softmax_bounded, values_full)

    @pl.when(jnp.logical_and(j == i, bounded))
    def _():
        step(scores_diag, softmax_bounded, values_diag)

    @pl.when(jnp.logical_and(j < i, jnp.logical_not(bounded)))
    def _():
        step(scores_full, softmax_full, values_full)

    @pl.when(jnp.logical_and(j == i, jnp.logical_not(bounded)))
    def _():
        step(scores_diag, softmax_diag, values_diag)

    @pl.when(j == i)
    def _():
        o = jnp.concatenate(
            [(acc_ref[h * V_HEAD_DIM:(h + 1) * V_HEAD_DIM, :] / l_ref[h]).T.astype(BF16)
             for h in range(N_HEADS)], axis=-1)
        y = jnp.dot(o, wo_ref[...], preferred_element_type=F32)
        z = DEEPNORM_ALPHA * x_ref[0] + gate_ref[0] * y
        o_ref[0] = _layer_norm(z, g_ref[...], b_ref[...])


def _score_bounded(q_norm2, k_norm2):
    q_max = jnp.sqrt(jnp.max(q_norm2, axis=-1))
    k_max = lax.cummax(jnp.sqrt(jnp.max(k_norm2, axis=-1)), axis=1)
    return (jnp.max(q_max * k_max, axis=-1) <= SCORE_BOUND).astype(jnp.int32)


def _attn_layer(x, qt, k, vt, bounded, gate, w_out, ln_g, ln_b):
    b, s, d = x.shape
    t = min(s, ATTN_TILE)
    nq = s // t
    pairs = [(i, j) for i in range(nq) for j in range(i + 1)]
    qi_np = np.array([p[0] for p in pairs], np.int32)
    qi = jnp.asarray(qi_np)
    kj = jnp.asarray(np.array([p[1] for p in pairs], np.int32))
    bounded_steps = bounded[:, qi_np].reshape(-1)
    grid_spec = pltpu.PrefetchScalarGridSpec(
        num_scalar_prefetch=3,
        grid=(b, len(pairs)),
        in_specs=[
            pl.BlockSpec((1, N_HEADS * HEAD_W, t), lambda bi, p, qi, kj, bd: (bi, 0, qi[p])),
            pl.BlockSpec((1, t, N_HEADS * HEAD_W), lambda bi, p, qi, kj, bd: (bi, kj[p], 0)),
            pl.BlockSpec((1, N_HEADS * V_HEAD_DIM, t), lambda bi, p, qi, kj, bd: (bi, 0, kj[p])),
            pl.BlockSpec((1, t, d), lambda bi, p, qi, kj, bd: (bi, qi[p], 0)),
            pl.BlockSpec((1, 1, d), lambda bi, p, qi, kj, bd: (bi, 0, 0)),
            _const_spec((d, d)), _const_spec((1, d)), _const_spec((1, d)),
        ],
        out_specs=pl.BlockSpec((1, t, d), lambda bi, p, qi, kj, bd: (bi, qi[p], 0)),
        scratch_shapes=[
            pltpu.VMEM((N_HEADS, 1, t), F32),
            pltpu.VMEM((N_HEADS, 1, t), F32),
            pltpu.VMEM((N_HEADS * V_HEAD_DIM, t), F32),
        ],
    )
    return pl.pallas_call(
        _attn_kernel,
        out_shape=jax.ShapeDtypeStruct((b, s, d), F32),
        grid_spec=grid_spec,
        compiler_params=_params(("arbitrary", "arbitrary")),
        name="attn_layer",
    )(qi, kj, bounded_steps, qt, k, vt, x, gate, w_out, ln_g, ln_b)


def _swap_halves(w):
    half = w.shape[-1] // 2
    return jnp.concatenate([w[..., half:], w[..., :half]], axis=-1)


def _split_q_up(w):
    r = w.shape[0]
    w = w.reshape(r, N_HEADS, QK_NOPE_DIM + QK_ROPE_DIM)
    nope = w[..., :QK_NOPE_DIM].reshape(r, N_HEADS * QK_NOPE_DIM)
    rope = w[..., QK_NOPE_DIM:]
    rope2 = jnp.concatenate([rope, _swap_halves(rope)], axis=-1).reshape(r, N_HEADS * LANES)
    return nope.T.astype(BF16), rope2.T.astype(BF16)


def kernel(x, c, positions, ada_w, ada_b, ln_g, ln_b, mlp_w1, mlp_w2, pool_w, pool_scale,
           q_down_w, q_norm_g, q_up_w, attn_out_w, kv_in_w, kv_norm_g, k_up_w, v_up_w):
    b, s, d = x.shape
    mod = _adaln_mod(c, ada_w, ada_b).reshape(DEPTH, b, N_MOD, 1, d)
    tab, tabt = _rope_table(positions)

    kv_rope_w = kv_in_w[:, KV_LORA_RANK:]
    kv_w_c = kv_in_w[:, :KV_LORA_RANK].astype(BF16)
    kv_w_r2 = jnp.concatenate([kv_rope_w, _swap_halves(kv_rope_w)], axis=-1).astype(BF16)

    k = vt = k_n2 = None
    for l in range(DEPTH):
        shift1, scale1, gate1, shift2, scale2, gate2 = (mod[l, :, m] for m in range(N_MOD))
        g1, b1 = ln_g[l, 0].reshape(1, d), ln_b[l, 0].reshape(1, d)
        g2, b2 = ln_g[l, 1].reshape(1, d), ln_b[l, 1].reshape(1, d)
        if l < N_A_LAYERS:
            x = _pool_layer(x, shift1, scale1, gate1, pool_w[l].astype(BF16),
                            pool_scale[l].reshape(1, d), g1, b1)
        else:
            if l == N_A_LAYERS:
                k, vt, k_n2 = _kv_proj(x, tab, kv_w_c, kv_w_r2,
                                       kv_norm_g.reshape(1, KV_LORA_RANK),
                                       k_up_w.astype(BF16), v_up_w.T.astype(BF16))
            jl = l - N_A_LAYERS
            w_nope_t, w_rope2_t = _split_q_up(q_up_w[jl])
            qt, q_n2 = _q_proj(x, tabt, shift1, scale1, q_down_w[jl].astype(BF16),
                               q_norm_g[jl].reshape(1, Q_LORA_RANK), w_nope_t, w_rope2_t)
            x = _attn_layer(x, qt, k, vt, _score_bounded(q_n2, k_n2), gate1,
                            attn_out_w[jl].astype(BF16), g1, b1)
        x = _mlp_layer(x, shift2, scale2, gate2, mlp_w1[l].astype(BF16), mlp_w2[l].astype(BF16),
                       g2, b2)
    return x
```

```python
import functools

import numpy as np
import jax
import jax.numpy as jnp
from jax import lax
from jax.experimental import pallas as pl
from jax.experimental.pallas import tpu as pltpu

D_MODEL = 1024
DEPTH = 4
N_A_LAYERS = DEPTH // 2
POOL_WINDOWS = (2, 4, 8, 16)
N_POOL_GROUPS = len(POOL_WINDOWS)
POOL_GROUP_DIM = D_MODEL // N_POOL_GROUPS
POOL_HALO = 16
QK_NOPE_DIM = 128
QK_ROPE_DIM = 64
V_HEAD_DIM = 128
N_HEADS = D_MODEL // 128
Q_LORA_RANK = D_MODEL // 2
KV_LORA_RANK = D_MODEL // 4
ROPE_THETA = 10000.0
D_FF = 4 * D_MODEL
N_MOD = 6
DEEPNORM_ALPHA = (2.0 * DEPTH) ** 0.25
LN_EPS = 1e-5
RMS_EPS = 1e-6
ATTN_SCALE = (QK_NOPE_DIM + QK_ROPE_DIM) ** -0.5
Q_SCALE = ATTN_SCALE * float(np.log2(np.e))

LANES = 128
BF16_SUBLANES = 16
ATTN_TILE = 512
SCORE_BOUND = 60.0
HEAD_W = 2 * LANES
ROPE_HALF = QK_ROPE_DIM // 2

VMEM_LIMIT = 56 * 1024 * 1024

F32 = jnp.float32
BF16 = jnp.bfloat16

_NT_DIMS = (((1,), (1,)), ((), ()))


def _params(semantics):
    return pltpu.CompilerParams(dimension_semantics=semantics, vmem_limit_bytes=VMEM_LIMIT)


def _const_spec(shape):
    nd = len(shape)
    return pl.BlockSpec(shape, lambda *_: (0,) * nd, pipeline_mode=pl.Buffered(1))


def _layer_norm(z, g, b):
    mu = jnp.mean(z, axis=-1, keepdims=True)
    zc = z - mu
    var = jnp.mean(zc * zc, axis=-1, keepdims=True)
    return zc * lax.rsqrt(var + LN_EPS) * g + b


def _zero_after(v):
    bits = lax.bitcast_convert_type(v, jnp.uint32)
    bits = lax.shift_right_logical(lax.shift_right_logical(bits, jnp.uint32(16)), jnp.uint32(16))
    return lax.bitcast_convert_type(bits, F32)


def _rope_pair_sum(y):
    return y + pltpu.roll(y, QK_ROPE_DIM, axis=1)


def _adaln_kernel(c_ref, w_ref, b_ref, o_ref):
    c = c_ref[...]
    sc = c * jax.nn.sigmoid(c)
    o_ref[0] = jnp.dot(sc.astype(BF16), w_ref[0].astype(BF16),
                       preferred_element_type=F32) + b_ref[0]


def _adaln_mod(c, ada_w, ada_b):
    b, d = c.shape
    n = ada_w.shape[-1]
    tn = 1536
    return pl.pallas_call(
        _adaln_kernel,
        out_shape=jax.ShapeDtypeStruct((DEPTH, b, n), F32),
        grid=(DEPTH, n // tn),
        in_specs=[
            pl.BlockSpec((b, d), lambda l, j: (0, 0)),
            pl.BlockSpec((1, d, tn), lambda l, j: (l, 0, j)),
            pl.BlockSpec((1, 1, tn), lambda l, j: (l, 0, j)),
        ],
        out_specs=pl.BlockSpec((1, b, tn), lambda l, j: (l, 0, j)),
        compiler_params=_params(("arbitrary", "arbitrary")),
        name="adaln_mod",
    )(c, ada_w, ada_b.reshape(DEPTH, 1, n))


def _rope_table_kernel(pos_ref, freq_ref, o_ref, ot_ref):
    pos = pos_ref[0].astype(F32)
    ang = freq_ref[...] * pos
    cos = jnp.cos(ang)
    sin = jnp.sin(ang)
    tab = jnp.concatenate([cos, cos, -sin, sin], axis=0)
    ot_ref[0] = tab
    o_ref[0] = tab.T


def _rope_table(positions):
    b, s = positions.shape
    t = min(s, 1024)
    inv_freq = ROPE_THETA ** (-jnp.arange(0, QK_ROPE_DIM, 2, dtype=F32) / QK_ROPE_DIM)
    return pl.pallas_call(
        _rope_table_kernel,
        out_shape=(jax.ShapeDtypeStruct((b, s, LANES), F32),
                   jax.ShapeDtypeStruct((b, LANES, s), F32)),
        grid=(b, s // t),
        in_specs=[
            pl.BlockSpec((1, 1, t), lambda bi, i: (bi, 0, i)),
            pl.BlockSpec((ROPE_HALF, 1), lambda bi, i: (0, 0)),
        ],
        out_specs=(pl.BlockSpec((1, t, LANES), lambda bi, i: (bi, i, 0)),
                   pl.BlockSpec((1, LANES, t), lambda bi, i: (bi, 0, i))),
        compiler_params=_params(("arbitrary", "arbitrary")),
        name="rope_table",
    )(positions.reshape(b, 1, s), inv_freq.reshape(ROPE_HALF, 1))


def _pool_kernel(x_ref, xh_ref, shift_ref, scale_ref, gate_ref, w_ref, ps_ref, o_ref):
    i = pl.program_id(1)
    t = x_ref.shape[1]
    x = x_ref[0]
    mul = 1.0 + scale_ref[0]
    shift = shift_ref[0]
    h = x * mul + shift
    hh = jnp.where(i == 0, 0.0, xh_ref[0] * mul + shift)
    ext = jnp.concatenate([hh, h], axis=0)

    row = i * t + lax.broadcasted_iota(jnp.int32, (t, 1), 0)
    ys = []
    level = ext
    for g, w in enumerate(POOL_WINDOWS):
        level = level + pltpu.roll(level, w // 2, axis=0)
        cols = slice(g * POOL_GROUP_DIM, (g + 1) * POOL_GROUP_DIM)
        cnt = jnp.minimum(row + 1, w).astype(F32)
        y = level[POOL_HALO:, :POOL_GROUP_DIM] / cnt - h[:, cols]
        ys.append(jnp.dot(y.astype(BF16), w_ref[g], preferred_element_type=F32))
        level = level[:, POOL_GROUP_DIM:]
    y = jnp.concatenate(ys, axis=-1) * ps_ref[...]
    o_ref[0] = DEEPNORM_ALPHA * x + gate_ref[0] * y


def _pool_layer(x, shift, scale, gate, w_pool, pool_scale):
    b, s, d = x.shape
    t = min(s, 512)
    halo_blocks = t // POOL_HALO
    mod_spec = pl.BlockSpec((1, 1, d), lambda bi, i: (bi, 0, 0))
    return pl.pallas_call(
        _pool_kernel,
        out_shape=jax.ShapeDtypeStruct((b, s, d), F32),
        grid=(b, s // t),
        in_specs=[
            pl.BlockSpec((1, t, d), lambda bi, i: (bi, i, 0)),
            pl.BlockSpec((1, POOL_HALO, d),
                         lambda bi, i: (bi, jnp.maximum(i * halo_blocks - 1, 0), 0)),
            mod_spec, mod_spec, mod_spec,
            _const_spec((N_POOL_GROUPS, POOL_GROUP_DIM, POOL_GROUP_DIM)),
            _const_spec((1, d)),
        ],
        out_specs=pl.BlockSpec((1, t, d), lambda bi, i: (bi, i, 0)),
        compiler_params=_params(("arbitrary", "arbitrary")),
        name="pool_layer",
    )(x, x, shift, scale, gate, w_pool, pool_scale)


def _mlp_kernel(z_ref, g1_ref, b1_ref, shift_ref, scale_ref, gate_ref, w1_ref, w2_ref,
                g2_ref, b2_ref, o_ref, x1_ref, h_ref, *, ff_chunk, sub_rows):
    n_sub = z_ref.shape[1] // sub_rows
    n_chunks = D_FF // ff_chunk
    part_rows = sub_rows // n_chunks
    mul = 1.0 + scale_ref[0]
    shift = shift_ref[0]

    def enter(r, lo, hi, after=None):
        rows = slice(r * sub_rows + lo, r * sub_rows + hi)
        g1 = g1_ref[...]
        if after is not None:
            g1 = g1 + _zero_after(after[0:1, :])
        x1 = _layer_norm(z_ref[0, rows, :], g1, b1_ref[...])
        x1_ref[rows, :] = x1
        h_ref[rows, :] = (x1 * mul + shift).astype(BF16)

    def ff_part(h, c):
        cols = slice(c * ff_chunk, (c + 1) * ff_chunk)
        a = jnp.maximum(jnp.dot(h, w1_ref[:, cols], preferred_element_type=F32), 0.0)
        return jnp.dot((a * a).astype(BF16), w2_ref[cols, :], preferred_element_type=F32)

    def finish(r, y, lo, hi, after=None):
        rows = slice(r * sub_rows + lo, r * sub_rows + hi)
        gate = gate_ref[0]
        if after is not None:
            gate = gate + _zero_after(after[0:1, :])
        z2 = DEEPNORM_ALPHA * x1_ref[rows, :] + gate * y[lo:hi]
        o_ref[0, rows, :] = _layer_norm(z2, g2_ref[...], b2_ref[...])

    enter(0, 0, sub_rows)
    prev = None
    for r in range(n_sub):
        h = h_ref[r * sub_rows:(r + 1) * sub_rows, :]
        y = None
        for c in range(n_chunks):
            part = ff_part(h, c)
            y = part if y is None else y + part
            lo, hi = c * part_rows, (c + 1) * part_rows
            if r + 1 < n_sub:
                enter(r + 1, lo, hi, after=part)
            if prev is not None:
                finish(r - 1, prev, lo, hi, after=part)
        prev = y
    finish(n_sub - 1, prev, 0, sub_rows)


def _mlp_layer(z, ln1_g, ln1_b, shift, scale, gate, w1, w2, ln2_g, ln2_b):
    b, s, d = z.shape
    t = min(s, 1024)
    mod_spec = pl.BlockSpec((1, 1, d), lambda bi, i: (bi, 0, 0))
    return pl.pallas_call(
        functools.partial(_mlp_kernel, ff_chunk=1024, sub_rows=min(t, 256)),
        out_shape=jax.ShapeDtypeStruct((b, s, d), F32),
        grid=(b, s // t),
        in_specs=[
            pl.BlockSpec((1, t, d), lambda bi, i: (bi, i, 0)),
            _const_spec((1, d)), _const_spec((1, d)),
            mod_spec, mod_spec, mod_spec,
            _const_spec((d, D_FF)), _const_spec((D_FF, d)),
            _const_spec((1, d)), _const_spec((1, d)),
        ],
        out_specs=pl.BlockSpec((1, t, d), lambda bi, i: (bi, i, 0)),
        scratch_shapes=[pltpu.VMEM((t, d), F32), pltpu.VMEM((t, d), BF16)],
        compiler_params=_params(("arbitrary", "arbitrary")),
        name="mlp_layer",
    )(z, ln1_g, ln1_b, shift, scale, gate, w1, w2, ln2_g, ln2_b)


def _kv_kernel(x_ref, tab_ref, wc_ref, wr_ref, ng_ref, kup_ref, vup_ref, k_ref, vt_ref, n2_ref):
    xb = x_ref[0].astype(BF16)
    c = jnp.dot(xb, wc_ref[...], preferred_element_type=F32)
    ms = jnp.mean(c * c, axis=-1, keepdims=True)
    ckv = (c * lax.rsqrt(ms + RMS_EPS) * ng_ref[...]).astype(BF16)
    y = jnp.dot(xb, wr_ref[...], preferred_element_type=F32) * tab_ref[0]
    lane = lax.broadcasted_iota(jnp.int32, y.shape, 1)
    kr = jnp.where(lane < QK_ROPE_DIM, _rope_pair_sum(y), 0.0)
    kn = jnp.dot(ckv, kup_ref[...], preferred_element_type=F32)
    kr_n2 = jnp.sum(kr * kr, axis=1, keepdims=True)
    n2 = []
    for h in range(N_HEADS):
        kn_h = kn[:, h * LANES:(h + 1) * LANES]
        k_ref[0, :, h * HEAD_W:h * HEAD_W + LANES] = kn_h.astype(BF16)
        k_ref[0, :, h * HEAD_W + LANES:(h + 1) * HEAD_W] = kr.astype(BF16)
        key_n2 = jnp.sum(kn_h * kn_h, axis=1, keepdims=True) + kr_n2
        n2.append(jnp.broadcast_to(jnp.max(key_n2, axis=0, keepdims=True), (1, LANES)))
    n2_ref[0, 0] = jnp.concatenate(n2, axis=0)
    vt_ref[0] = lax.dot_general(vup_ref[...], ckv, _NT_DIMS,
                                preferred_element_type=F32).astype(BF16)


def _kv_proj(x, tab, w_c, w_r2, norm_g, k_up, v_up_t):
    b, s, d = x.shape
    t = min(s, ATTN_TILE)
    return pl.pallas_call(
        _kv_kernel,
        out_shape=(jax.ShapeDtypeStruct((b, s, N_HEADS * HEAD_W), BF16),
                   jax.ShapeDtypeStruct((b, N_HEADS * V_HEAD_DIM, s), BF16),
                   jax.ShapeDtypeStruct((b, s // t, N_HEADS, LANES), F32)),
        grid=(b, s // t),
        in_specs=[
            pl.BlockSpec((1, t, d), lambda bi, i: (bi, i, 0)),
            pl.BlockSpec((1, t, LANES), lambda bi, i: (bi, i, 0)),
            _const_spec((d, KV_LORA_RANK)), _const_spec((d, LANES)),
            _const_spec((1, KV_LORA_RANK)),
            _const_spec((KV_LORA_RANK, N_HEADS * QK_NOPE_DIM)),
            _const_spec((N_HEADS * V_HEAD_DIM, KV_LORA_RANK)),
        ],
        out_specs=(pl.BlockSpec((1, t, N_HEADS * HEAD_W), lambda bi, i: (bi, i, 0)),
                   pl.BlockSpec((1, N_HEADS * V_HEAD_DIM, t), lambda bi, i: (bi, 0, i)),
                   pl.BlockSpec((1, 1, N_HEADS, LANES), lambda bi, i: (bi, i, 0, 0))),
        compiler_params=_params(("arbitrary", "arbitrary")),
        name="kv_proj",
    )(x, tab, w_c, w_r2, norm_g, k_up, v_up_t)


def _q_kernel(x_ref, tabt_ref, shift_ref, scale_ref, wd_ref, ng_ref, wn_ref, wr_ref, qt_ref,
              n2_ref):
    h = (x_ref[0] * (1.0 + scale_ref[0]) + shift_ref[0]).astype(BF16)
    c = jnp.dot(h, wd_ref[...], preferred_element_type=F32)
    ms = jnp.mean(c * c, axis=-1, keepdims=True)
    cq = (c * lax.rsqrt(ms + RMS_EPS) * ng_ref[...]).astype(BF16)
    qn = lax.dot_general(wn_ref[...], cq, _NT_DIMS, preferred_element_type=F32)
    qr = lax.dot_general(wr_ref[...], cq, _NT_DIMS, preferred_element_type=F32)
    tabt = tabt_ref[0]
    t = tabt.shape[1]
    n2 = []
    for hd in range(N_HEADS):
        rows = slice(hd * LANES, (hd + 1) * LANES)
        nope = qn[rows] * Q_SCALE
        qt_ref[0, hd * HEAD_W:hd * HEAD_W + LANES, :] = nope.astype(BF16)
        y = qr[rows] * tabt
        rot = (y[:QK_ROPE_DIM] + y[QK_ROPE_DIM:]) * Q_SCALE
        qt_ref[0, hd * HEAD_W + LANES:hd * HEAD_W + LANES + QK_ROPE_DIM, :] = rot.astype(BF16)
        qt_ref[0, hd * HEAD_W + LANES + QK_ROPE_DIM:(hd + 1) * HEAD_W, :] = rot.astype(BF16)
        query_n2 = (jnp.sum(nope * nope, axis=0, keepdims=True)
                    + jnp.sum(rot * rot, axis=0, keepdims=True))
        folded = query_n2[:, :LANES]
        for c in range(1, t // LANES):
            folded = jnp.maximum(folded, query_n2[:, c * LANES:(c + 1) * LANES])
        n2.append(folded)
    n2_ref[0, 0] = jnp.concatenate(n2, axis=0)


def _q_proj(x, tabt, shift, scale, w_down, norm_g, w_nope_t, w_rope2_t):
    b, s, d = x.shape
    t = min(s, ATTN_TILE)
    mod_spec = pl.BlockSpec((1, 1, d), lambda bi, i: (bi, 0, 0))
    return pl.pallas_call(
        _q_kernel,
        out_shape=(jax.ShapeDtypeStruct((b, N_HEADS * HEAD_W, s), BF16),
                   jax.ShapeDtypeStruct((b, s // t, N_HEADS, LANES), F32)),
        grid=(b, s // t),
        in_specs=[
            pl.BlockSpec((1, t, d), lambda bi, i: (bi, i, 0)),
            pl.BlockSpec((1, LANES, t), lambda bi, i: (bi, 0, i)),
            mod_spec, mod_spec,
            _const_spec((d, Q_LORA_RANK)), _const_spec((1, Q_LORA_RANK)),
            _const_spec((N_HEADS * LANES, Q_LORA_RANK)),
            _const_spec((N_HEADS * LANES, Q_LORA_RANK)),
        ],
        out_specs=(pl.BlockSpec((1, N_HEADS * HEAD_W, t), lambda bi, i: (bi, 0, i)),
                   pl.BlockSpec((1, 1, N_HEADS, LANES), lambda bi, i: (bi, i, 0, 0))),
        compiler_params=_params(("arbitrary", "arbitrary")),
        name="q_proj",
    )(x, tabt, shift, scale, w_down, norm_g, w_nope_t, w_rope2_t)


def _attn_kernel(qi_ref, kj_ref, bounded_ref, qt_ref, k_ref, vt_ref, x_ref, gate_ref, wo_ref,
                 o_ref, m_ref, l_ref, acc_ref):
    p = pl.program_id(1)
    i = qi_ref[p]
    j = kj_ref[p]
    bounded = bounded_ref[pl.program_id(0) * pl.num_programs(1) + p] != 0
    tk = k_ref.shape[1]

    @pl.when(j == 0)
    def _():
        m_ref[...] = jnp.full(m_ref.shape, -jnp.inf, F32)
        l_ref[...] = jnp.zeros(l_ref.shape, F32)
        acc_ref[...] = jnp.zeros(acc_ref.shape, F32)

    half = tk // 2
    ones_rows = jnp.ones((BF16_SUBLANES, tk), BF16)

    def head_operands(h):
        return (k_ref[0, :, h * HEAD_W:(h + 1) * HEAD_W],
                qt_ref[0, h * HEAD_W:(h + 1) * HEAD_W, :])

    def scores_full(h):
        k, qt = head_operands(h)
        return (jnp.dot(k, qt, preferred_element_type=F32),)

    def softmax_full(h, st):
        m_prev = m_ref[h]
        m_new = jnp.maximum(m_prev, jnp.max(st, axis=0, keepdims=True))
        m_ref[h] = m_new
        return jnp.exp2(m_prev - m_new), jnp.exp2((st - m_new).astype(BF16))

    def softmax_bounded(h, *sts):
        return (None,) + tuple(jnp.exp2(st).astype(BF16) for st in sts)

    def values_full(h, vt_ones, e):
        return jnp.dot(vt_ones, e, preferred_element_type=F32)

    def scores_diag(h):
        k, qt = head_operands(h)
        tri = (lax.broadcasted_iota(jnp.int32, (half, half), 0)
               <= lax.broadcasted_iota(jnp.int32, (half, half), 1))
        st_lo = jnp.dot(k[:half], qt, preferred_element_type=F32)
        st_hi = jnp.dot(k[half:], qt[:, half:], preferred_element_type=F32)
        st_lo = jnp.concatenate([jnp.where(tri, st_lo[:, :half], -jnp.inf), st_lo[:, half:]],
                                axis=1)
        return st_lo, jnp.where(tri, st_hi, -jnp.inf)

    def softmax_diag(h, st_lo, st_hi):
        m_prev = m_ref[h]
        mx_lo = jnp.max(st_lo, axis=0, keepdims=True)
        mx_hi = jnp.max(st_hi, axis=0, keepdims=True)
        m_blk = jnp.concatenate([mx_lo[:, :half], jnp.maximum(mx_lo[:, half:], mx_hi)], axis=1)
        m_new = jnp.maximum(m_prev, m_blk)
        m_ref[h] = m_new
        return (jnp.exp2(m_prev - m_new), jnp.exp2((st_lo - m_new).astype(BF16)),
                jnp.exp2((st_hi - m_new[:, half:]).astype(BF16)))

    def values_diag(h, vt_ones, e_lo, e_hi):
        pv_lo = jnp.dot(vt_ones[:, :half], e_lo, preferred_element_type=F32)
        pv_hi = jnp.dot(vt_ones[:, half:], e_hi, preferred_element_type=F32)
        return jnp.concatenate([pv_lo[:, :half], pv_lo[:, half:] + pv_hi], axis=1)

    def step(scores, softmax, values):
        def accumulate(h, alpha, *e):
            rows = slice(h * V_HEAD_DIM, (h + 1) * V_HEAD_DIM)
            vt_ones = jnp.concatenate([vt_ref[0, rows, :], ones_rows], axis=0)
            pv = values(h, vt_ones, *e)
            if alpha is None:
                acc_ref[rows, :] += pv[:V_HEAD_DIM]
                l_ref[h] += pv[V_HEAD_DIM:V_HEAD_DIM + 1]
            else:
                acc_ref[rows, :] = alpha * acc_ref[rows, :] + pv[:V_HEAD_DIM]
                l_ref[h] = alpha * l_ref[h] + pv[V_HEAD_DIM:V_HEAD_DIM + 1]

        st = {0: scores(0), 1: scores(1)}
        pe = {0: softmax(0, *st.pop(0))}
        for h in range(N_HEADS):
            accumulate(h, *pe.pop(h))
            if h + 2 < N_HEADS:
                st[h + 2] = scores(h + 2)
            if h + 1 < N_HEADS:
                pe[h + 1] = softmax(h + 1, *st.pop(h + 1))

    @pl.when(jnp.logical_and(j < i, bounded))
    def _():
        step(scores_full, softmax_bounded, values_full)

    @pl.when(jnp.logical_and(j == i, bounded))
    def _():
        step(scores_diag, softmax_bounded, values_diag)

    @pl.when(jnp.logical_and(j < i, jnp.logical_not(bounded)))
    def _():
        step(scores_full, softmax_full, values_full)

    @pl.when(jnp.logical_and(j == i, jnp.logical_not(bounded)))
    def _():
        step(scores_diag, softmax_diag, values_diag)

    @pl.when(j == i)
    def _():
        o = jnp.concatenate(
            [(acc_ref[h * V_HEAD_DIM:(h + 1) * V_HEAD_DIM, :] / l_ref[h]).T.astype(BF16)
             for h in range(N_HEADS)], axis=-1)
        y = jnp.dot(o, wo_ref[...], preferred_element_type=F32)
        o_ref[0] = DEEPNORM_ALPHA * x_ref[0] + gate_ref[0] * y


def _score_bounded(q_norm2, k_norm2):
    q_max = jnp.sqrt(jnp.max(q_norm2, axis=-1))
    k_max = lax.cummax(jnp.sqrt(jnp.max(k_norm2, axis=-1)), axis=1)
    return (jnp.max(q_max * k_max, axis=-1) <= SCORE_BOUND).astype(jnp.int32)


def _attn_layer(x, qt, k, vt, bounded, gate, w_out):
    b, s, d = x.shape
    t = min(s, ATTN_TILE)
    nq = s // t
    pairs = [(i, j) for i in range(nq) for j in range(i + 1)]
    qi_np = np.array([p[0] for p in pairs], np.int32)
    qi = jnp.asarray(qi_np)
    kj = jnp.asarray(np.array([p[1] for p in pairs], np.int32))
    bounded_steps = bounded[:, qi_np].reshape(-1)
    grid_spec = pltpu.PrefetchScalarGridSpec(
        num_scalar_prefetch=3,
        grid=(b, len(pairs)),
        in_specs=[
            pl.BlockSpec((1, N_HEADS * HEAD_W, t), lambda bi, p, qi, kj, bd: (bi, 0, qi[p])),
            pl.BlockSpec((1, t, N_HEADS * HEAD_W), lambda bi, p, qi, kj, bd: (bi, kj[p], 0)),
            pl.BlockSpec((1, N_HEADS * V_HEAD_DIM, t), lambda bi, p, qi, kj, bd: (bi, 0, kj[p])),
            pl.BlockSpec((1, t, d), lambda bi, p, qi, kj, bd: (bi, qi[p], 0)),
            pl.BlockSpec((1, 1, d), lambda bi, p, qi, kj, bd: (bi, 0, 0)),
            _const_spec((d, d)),
        ],
        out_specs=pl.BlockSpec((1, t, d), lambda bi, p, qi, kj, bd: (bi, qi[p], 0)),
        scratch_shapes=[
            pltpu.VMEM((N_HEADS, 1, t), F32),
            pltpu.VMEM((N_HEADS, 1, t), F32),
            pltpu.VMEM((N_HEADS * V_HEAD_DIM, t), F32),
        ],
    )
    return pl.pallas_call(
        _attn_kernel,
        out_shape=jax.ShapeDtypeStruct((b, s, d), F32),
        grid_spec=grid_spec,
        compiler_params=_params(("arbitrary", "arbitrary")),
        name="attn_layer",
    )(qi, kj, bounded_steps, qt, k, vt, x, gate, w_out)


def _swap_halves(w):
    half = w.shape[-1] // 2
    return jnp.concatenate([w[..., half:], w[..., :half]], axis=-1)


def _split_q_up(w):
    r = w.shape[0]
    w = w.reshape(r, N_HEADS, QK_NOPE_DIM + QK_ROPE_DIM)
    nope = w[..., :QK_NOPE_DIM].reshape(r, N_HEADS * QK_NOPE_DIM)
    rope = w[..., QK_NOPE_DIM:]
    rope2 = jnp.concatenate([rope, _swap_halves(rope)], axis=-1).reshape(r, N_HEADS * LANES)
    return nope.T.astype(BF16), rope2.T.astype(BF16)


def kernel(x, c, positions, ada_w, ada_b, ln_g, ln_b, mlp_w1, mlp_w2, pool_w, pool_scale,
           q_down_w, q_norm_g, q_up_w, attn_out_w, kv_in_w, kv_norm_g, k_up_w, v_up_w):
    b, s, d = x.shape
    mod = _adaln_mod(c, ada_w, ada_b).reshape(DEPTH, b, N_MOD, 1, d)
    tab, tabt = _rope_table(positions)

    kv_rope_w = kv_in_w[:, KV_LORA_RANK:]
    kv_w_c = kv_in_w[:, :KV_LORA_RANK].astype(BF16)
    kv_w_r2 = jnp.concatenate([kv_rope_w, _swap_halves(kv_rope_w)], axis=-1).astype(BF16)

    k = vt = k_n2 = None
    for l in range(DEPTH):
        shift1, scale1, gate1, shift2, scale2, gate2 = (mod[l, :, m] for m in range(N_MOD))
        g1, b1 = ln_g[l, 0].reshape(1, d), ln_b[l, 0].reshape(1, d)
        g2, b2 = ln_g[l, 1].reshape(1, d), ln_b[l, 1].reshape(1, d)
        if l < N_A_LAYERS:
            z = _pool_layer(x, shift1, scale1, gate1, pool_w[l].astype(BF16),
                            pool_scale[l].reshape(1, d))
        else:
            if l == N_A_LAYERS:
                k, vt, k_n2 = _kv_proj(x, tab, kv_w_c, kv_w_r2,
                                       kv_norm_g.reshape(1, KV_LORA_RANK),
                                       k_up_w.astype(BF16), v_up_w.T.astype(BF16))
            jl = l - N_A_LAYERS
            w_nope_t, w_rope2_t = _split_q_up(q_up_w[jl])
            qt, q_n2 = _q_proj(x, tabt, shift1, scale1, q_down_w[jl].astype(BF16),
                               q_norm_g[jl].reshape(1, Q_LORA_RANK), w_nope_t, w_rope2_t)
            z = _attn_layer(x, qt, k, vt, _score_bounded(q_n2, k_n2), gate1,
                            attn_out_w[jl].astype(BF16))
        x = _mlp_layer(z, g1, b1, shift2, scale2, gate2, mlp_w1[l].astype(BF16),
                       mlp_w2[l].astype(BF16), g2, b2)
    return x
```

```python
import functools

import numpy as np
import jax
import jax.numpy as jnp
from jax import lax
from jax.experimental import pallas as pl
from jax.experimental.pallas import tpu as pltpu

D_MODEL = 1024
DEPTH = 4
N_A_LAYERS = DEPTH // 2
POOL_WINDOWS = (2, 4, 8, 16)
N_POOL_GROUPS = len(POOL_WINDOWS)
POOL_GROUP_DIM = D_MODEL // N_POOL_GROUPS
POOL_HALO = 16
QK_NOPE_DIM = 128
QK_ROPE_DIM = 64
V_HEAD_DIM = 128
N_HEADS = D_MODEL // 128
Q_LORA_RANK = D_MODEL // 2
KV_LORA_RANK = D_MODEL // 4
ROPE_THETA = 10000.0
D_FF = 4 * D_MODEL
N_MOD = 6
DEEPNORM_ALPHA = (2.0 * DEPTH) ** 0.25
LN_EPS = 1e-5
RMS_EPS = 1e-6
ATTN_SCALE = (QK_NOPE_DIM + QK_ROPE_DIM) ** -0.5
Q_SCALE = ATTN_SCALE * float(np.log2(np.e))

LANES = 128
BF16_SUBLANES = 16
ATTN_TILE = 512
SCORE_BOUND = 60.0
NORM_SLACK = 1.0 + 2.0 ** -6
HEAD_W = 2 * LANES
ROPE_HALF = QK_ROPE_DIM // 2

VMEM_LIMIT = 56 * 1024 * 1024

F32 = jnp.float32
BF16 = jnp.bfloat16

_NT_DIMS = (((1,), (1,)), ((), ()))


def _params(semantics):
    return pltpu.CompilerParams(dimension_semantics=semantics, vmem_limit_bytes=VMEM_LIMIT)


def _const_spec(shape):
    nd = len(shape)
    return pl.BlockSpec(shape, lambda *_: (0,) * nd, pipeline_mode=pl.Buffered(1))


def _layer_norm(z, g, b):
    mu = jnp.mean(z, axis=-1, keepdims=True)
    zc = z - mu
    var = jnp.mean(zc * zc, axis=-1, keepdims=True)
    return zc * lax.rsqrt(var + LN_EPS) * g + b


def _zero_after(v):
    bits = lax.bitcast_convert_type(v, jnp.uint32)
    bits = lax.shift_right_logical(lax.shift_right_logical(bits, jnp.uint32(16)), jnp.uint32(16))
    return lax.bitcast_convert_type(bits, F32)


def _rope_pair_sum(y):
    return y + pltpu.roll(y, QK_ROPE_DIM, axis=1)


def _adaln_kernel(c_ref, w_ref, b_ref, o_ref):
    c = c_ref[...]
    sc = c * jax.nn.sigmoid(c)
    o_ref[0] = jnp.dot(sc.astype(BF16), w_ref[0].astype(BF16),
                       preferred_element_type=F32) + b_ref[0]


def _adaln_mod(c, ada_w, ada_b):
    b, d = c.shape
    n = ada_w.shape[-1]
    tn = 1536
    return pl.pallas_call(
        _adaln_kernel,
        out_shape=jax.ShapeDtypeStruct((DEPTH, b, n), F32),
        grid=(DEPTH, n // tn),
        in_specs=[
            pl.BlockSpec((b, d), lambda l, j: (0, 0)),
            pl.BlockSpec((1, d, tn), lambda l, j: (l, 0, j)),
            pl.BlockSpec((1, 1, tn), lambda l, j: (l, 0, j)),
        ],
        out_specs=pl.BlockSpec((1, b, tn), lambda l, j: (l, 0, j)),
        compiler_params=_params(("arbitrary", "arbitrary")),
        name="adaln_mod",
    )(c, ada_w, ada_b.reshape(DEPTH, 1, n))


def _rope_table_kernel(pos_ref, freq_ref, o_ref, ot_ref):
    pos = pos_ref[0].astype(F32)
    ang = freq_ref[...] * pos
    cos = jnp.cos(ang)
    sin = jnp.sin(ang)
    tab = jnp.concatenate([cos, cos, -sin, sin], axis=0)
    ot_ref[0] = tab
    o_ref[0] = tab.T


def _rope_table(positions):
    b, s = positions.shape
    t = min(s, 1024)
    inv_freq = ROPE_THETA ** (-jnp.arange(0, QK_ROPE_DIM, 2, dtype=F32) / QK_ROPE_DIM)
    return pl.pallas_call(
        _rope_table_kernel,
        out_shape=(jax.ShapeDtypeStruct((b, s, LANES), F32),
                   jax.ShapeDtypeStruct((b, LANES, s), F32)),
        grid=(b, s // t),
        in_specs=[
            pl.BlockSpec((1, 1, t), lambda bi, i: (bi, 0, i)),
            pl.BlockSpec((ROPE_HALF, 1), lambda bi, i: (0, 0)),
        ],
        out_specs=(pl.BlockSpec((1, t, LANES), lambda bi, i: (bi, i, 0)),
                   pl.BlockSpec((1, LANES, t), lambda bi, i: (bi, 0, i))),
        compiler_params=_params(("arbitrary", "arbitrary")),
        name="rope_table",
    )(positions.reshape(b, 1, s), inv_freq.reshape(ROPE_HALF, 1))


def _pool_kernel(x_ref, xh_ref, shift_ref, scale_ref, gate_ref, w_ref, ps_ref, o_ref):
    i = pl.program_id(1)
    t = x_ref.shape[1]
    x = x_ref[0]
    mul = 1.0 + scale_ref[0]
    shift = shift_ref[0]
    h = x * mul + shift
    hh = jnp.where(i == 0, 0.0, xh_ref[0] * mul + shift)
    ext = jnp.concatenate([hh, h], axis=0)

    row = i * t + lax.broadcasted_iota(jnp.int32, (t, 1), 0)
    ys = []
    level = ext
    for g, w in enumerate(POOL_WINDOWS):
        level = level + pltpu.roll(level, w // 2, axis=0)
        cols = slice(g * POOL_GROUP_DIM, (g + 1) * POOL_GROUP_DIM)
        cnt = jnp.minimum(row + 1, w).astype(F32)
        y = level[POOL_HALO:, :POOL_GROUP_DIM] / cnt - h[:, cols]
        ys.append(jnp.dot(y.astype(BF16), w_ref[g], preferred_element_type=F32))
        level = level[:, POOL_GROUP_DIM:]
    y = jnp.concatenate(ys, axis=-1) * ps_ref[...]
    o_ref[0] = DEEPNORM_ALPHA * x + gate_ref[0] * y


def _pool_layer(x, shift, scale, gate, w_pool, pool_scale):
    b, s, d = x.shape
    t = min(s, 1024)
    halo_blocks = t // POOL_HALO
    mod_spec = pl.BlockSpec((1, 1, d), lambda bi, i: (bi, 0, 0))
    return pl.pallas_call(
        _pool_kernel,
        out_shape=jax.ShapeDtypeStruct((b, s, d), F32),
        grid=(b, s // t),
        in_specs=[
            pl.BlockSpec((1, t, d), lambda bi, i: (bi, i, 0)),
            pl.BlockSpec((1, POOL_HALO, d),
                         lambda bi, i: (bi, jnp.maximum(i * halo_blocks - 1, 0), 0)),
            mod_spec, mod_spec, mod_spec,
            _const_spec((N_POOL_GROUPS, POOL_GROUP_DIM, POOL_GROUP_DIM)),
            _const_spec((1, d)),
        ],
        out_specs=pl.BlockSpec((1, t, d), lambda bi, i: (bi, i, 0)),
        compiler_params=_params(("arbitrary", "arbitrary")),
        name="pool_layer",
    )(x, x, shift, scale, gate, w_pool, pool_scale)


def _mlp_kernel(z_ref, g1_ref, b1_ref, shift_ref, scale_ref, gate_ref, w1_ref, w2_ref,
                g2_ref, b2_ref, o_ref, x1_ref, h_ref, *, ff_chunk, sub_rows):
    n_sub = z_ref.shape[1] // sub_rows
    n_chunks = D_FF // ff_chunk
    part_rows = sub_rows // n_chunks
    mul = 1.0 + scale_ref[0]
    shift = shift_ref[0]

    def enter(r, lo, hi, after=None):
        rows = slice(r * sub_rows + lo, r * sub_rows + hi)
        g1 = g1_ref[...]
        if after is not None:
            g1 = g1 + _zero_after(after[0:1, :])
        x1 = _layer_norm(z_ref[0, rows, :], g1, b1_ref[...])
        x1_ref[rows, :] = x1
        h_ref[rows, :] = (x1 * mul + shift).astype(BF16)

    def ff_part(h, c):
        cols = slice(c * ff_chunk, (c + 1) * ff_chunk)
        a = jnp.maximum(jnp.dot(h, w1_ref[:, cols], preferred_element_type=F32), 0.0)
        return jnp.dot((a * a).astype(BF16), w2_ref[cols, :], preferred_element_type=F32)

    def finish(r, y, lo, hi, after=None):
        rows = slice(r * sub_rows + lo, r * sub_rows + hi)
        gate = gate_ref[0]
        if after is not None:
            gate = gate + _zero_after(after[0:1, :])
        z2 = DEEPNORM_ALPHA * x1_ref[rows, :] + gate * y[lo:hi]
        o_ref[0, rows, :] = _layer_norm(z2, g2_ref[...], b2_ref[...])

    enter(0, 0, sub_rows)
    prev = None
    for r in range(n_sub):
        h = h_ref[r * sub_rows:(r + 1) * sub_rows, :]
        y = None
        for c in range(n_chunks):
            part = ff_part(h, c)
            y = part if y is None else y + part
            lo, hi = c * part_rows, (c + 1) * part_rows
            if r + 1 < n_sub:
                enter(r + 1, lo, hi, after=part)
            if prev is not None:
                finish(r - 1, prev, lo, hi, after=part)
        prev = y
    finish(n_sub - 1, prev, 0, sub_rows)


def _mlp_layer(z, ln1_g, ln1_b, shift, scale, gate, w1, w2, ln2_g, ln2_b):
    b, s, d = z.shape
    t = min(s, 1024)
    mod_spec = pl.BlockSpec((1, 1, d), lambda bi, i: (bi, 0, 0))
    return pl.pallas_call(
        functools.partial(_mlp_kernel, ff_chunk=1024, sub_rows=min(t, 256)),
        out_shape=jax.ShapeDtypeStruct((b, s, d), F32),
        grid=(b, s // t),
        in_specs=[
            pl.BlockSpec((1, t, d), lambda bi, i: (bi, i, 0)),
            _const_spec((1, d)), _const_spec((1, d)),
            mod_spec, mod_spec, mod_spec,
            _const_spec((d, D_FF)), _const_spec((D_FF, d)),
            _const_spec((1, d)), _const_spec((1, d)),
        ],
        out_specs=pl.BlockSpec((1, t, d), lambda bi, i: (bi, i, 0)),
        scratch_shapes=[pltpu.VMEM((t, d), F32), pltpu.VMEM((t, d), BF16)],
        compiler_params=_params(("arbitrary", "arbitrary")),
        name="mlp_layer",
    )(z, ln1_g, ln1_b, shift, scale, gate, w1, w2, ln2_g, ln2_b)


def _kv_kernel(x_ref, tab_ref, win_ref, ng_ref, kup_ref, vup_ref, ind_ref, k_ref, vt_ref, n2_ref):
    xb = x_ref[0].astype(BF16)
    cy = jnp.dot(xb, win_ref[...], preferred_element_type=F32)
    c = cy[:, :KV_LORA_RANK]
    ms = jnp.mean(c * c, axis=-1, keepdims=True)
    ckv = (c * lax.rsqrt(ms + RMS_EPS) * ng_ref[...]).astype(BF16)
    y = cy[:, KV_LORA_RANK:] * tab_ref[0]
    lane = lax.broadcasted_iota(jnp.int32, y.shape, 1)
    kr = jnp.where(lane < QK_ROPE_DIM, _rope_pair_sum(y), 0.0)
    kn = jnp.dot(ckv, kup_ref[...], preferred_element_type=F32)
    kr_bf = kr.astype(BF16)
    for h in range(N_HEADS):
        k_ref[0, :, h * HEAD_W:h * HEAD_W + LANES] = kn[:, h * LANES:(h + 1) * LANES].astype(BF16)
        k_ref[0, :, h * HEAD_W + LANES:(h + 1) * HEAD_W] = kr_bf
    nope_n2 = jnp.dot((kn * kn).astype(BF16), ind_ref[...], preferred_element_type=F32)
    key_n2 = (nope_n2 + jnp.sum(kr * kr, axis=1, keepdims=True)) * NORM_SLACK
    n2_ref[0, 0] = jnp.max(key_n2, axis=0, keepdims=True)
    vt_ref[0] = lax.dot_general(vup_ref[...], ckv, _NT_DIMS,
                                preferred_element_type=F32).astype(BF16)


def _kv_proj(x, tab, w_in, norm_g, k_up, v_up_t):
    b, s, d = x.shape
    t = min(s, ATTN_TILE)
    head_of_lane = np.arange(N_HEADS * QK_NOPE_DIM) // QK_NOPE_DIM
    indicator = jnp.asarray(head_of_lane[:, None] == np.arange(LANES)[None, :], BF16)
    return pl.pallas_call(
        _kv_kernel,
        out_shape=(jax.ShapeDtypeStruct((b, s, N_HEADS * HEAD_W), BF16),
                   jax.ShapeDtypeStruct((b, N_HEADS * V_HEAD_DIM, s), BF16),
                   jax.ShapeDtypeStruct((b, s // t, 1, LANES), F32)),
        grid=(b, s // t),
        in_specs=[
            pl.BlockSpec((1, t, d), lambda bi, i: (bi, i, 0)),
            pl.BlockSpec((1, t, LANES), lambda bi, i: (bi, i, 0)),
            _const_spec((d, KV_LORA_RANK + LANES)),
            _const_spec((1, KV_LORA_RANK)),
            _const_spec((KV_LORA_RANK, N_HEADS * QK_NOPE_DIM)),
            _const_spec((N_HEADS * V_HEAD_DIM, KV_LORA_RANK)),
            _const_spec((N_HEADS * QK_NOPE_DIM, LANES)),
        ],
        out_specs=(pl.BlockSpec((1, t, N_HEADS * HEAD_W), lambda bi, i: (bi, i, 0)),
                   pl.BlockSpec((1, N_HEADS * V_HEAD_DIM, t), lambda bi, i: (bi, 0, i)),
                   pl.BlockSpec((1, 1, 1, LANES), lambda bi, i: (bi, i, 0, 0))),
        compiler_params=_params(("arbitrary", "arbitrary")),
        name="kv_proj",
    )(x, tab, w_in, norm_g, k_up, v_up_t, indicator)


def _q_kernel(x_ref, tabt_ref, shift_ref, scale_ref, wd_ref, ng_ref, wq_ref, qt_ref, n2_ref):
    h = (x_ref[0] * (1.0 + scale_ref[0]) + shift_ref[0]).astype(BF16)
    c = jnp.dot(h, wd_ref[...], preferred_element_type=F32)
    ms = jnp.mean(c * c, axis=-1, keepdims=True)
    cq = (c * lax.rsqrt(ms + RMS_EPS) * ng_ref[...]).astype(BF16)
    qa = lax.dot_general(wq_ref[...], cq, _NT_DIMS, preferred_element_type=F32)
    tabt = tabt_ref[0]
    cos2, sin2 = tabt[:QK_ROPE_DIM], tabt[QK_ROPE_DIM:]
    t = tabt.shape[1]
    rope0 = N_HEADS * QK_NOPE_DIM
    n2 = []
    for hd in range(N_HEADS):
        nope = qa[hd * LANES:(hd + 1) * LANES] * Q_SCALE
        qt_ref[0, hd * HEAD_W:hd * HEAD_W + LANES, :] = nope.astype(BF16)
        a = qa[rope0 + hd * QK_ROPE_DIM:rope0 + (hd + 1) * QK_ROPE_DIM]
        swapped = jnp.concatenate([a[ROPE_HALF:], a[:ROPE_HALF]], axis=0)
        rot = (a * cos2 + swapped * sin2) * Q_SCALE
        qt_ref[0, hd * HEAD_W + LANES:hd * HEAD_W + LANES + QK_ROPE_DIM, :] = rot.astype(BF16)
        qt_ref[0, hd * HEAD_W + LANES + QK_ROPE_DIM:(hd + 1) * HEAD_W, :] = rot.astype(BF16)
        query_n2 = (jnp.sum(nope * nope, axis=0, keepdims=True)
                    + jnp.sum(rot * rot, axis=0, keepdims=True))
        folded = query_n2[:, :LANES]
        for c in range(1, t // LANES):
            folded = jnp.maximum(folded, query_n2[:, c * LANES:(c + 1) * LANES])
        n2.append(folded)
    n2_ref[0, 0] = jnp.concatenate(n2, axis=0)


def _q_proj(x, tabt, shift, scale, w_down, norm_g, w_q_t):
    b, s, d = x.shape
    t = min(s, ATTN_TILE)
    mod_spec = pl.BlockSpec((1, 1, d), lambda bi, i: (bi, 0, 0))
    return pl.pallas_call(
        _q_kernel,
        out_shape=(jax.ShapeDtypeStruct((b, N_HEADS * HEAD_W, s), BF16),
                   jax.ShapeDtypeStruct((b, s // t, N_HEADS, LANES), F32)),
        grid=(b, s // t),
        in_specs=[
            pl.BlockSpec((1, t, d), lambda bi, i: (bi, i, 0)),
            pl.BlockSpec((1, LANES, t), lambda bi, i: (bi, 0, i)),
            mod_spec, mod_spec,
            _const_spec((d, Q_LORA_RANK)), _const_spec((1, Q_LORA_RANK)),
            _const_spec((N_HEADS * (QK_NOPE_DIM + QK_ROPE_DIM), Q_LORA_RANK)),
        ],
        out_specs=(pl.BlockSpec((1, N_HEADS * HEAD_W, t), lambda bi, i: (bi, 0, i)),
                   pl.BlockSpec((1, 1, N_HEADS, LANES), lambda bi, i: (bi, i, 0, 0))),
        compiler_params=_params(("arbitrary", "arbitrary")),
        name="q_proj",
    )(x, tabt, shift, scale, w_down, norm_g, w_q_t)


def _attn_kernel(qi_ref, kj_ref, bounded_ref, qt_ref, k_ref, vt_ref, x_ref, gate_ref, wo_ref,
                 o_ref, m_ref, l_ref, acc_ref):
    p = pl.program_id(1)
    i = qi_ref[p]
    j = kj_ref[p]
    bounded = bounded_ref[pl.program_id(0) * pl.num_programs(1) + p] != 0
    tk = k_ref.shape[1]

    @pl.when(j == 0)
    def _():
        m_ref[...] = jnp.full(m_ref.shape, -jnp.inf, F32)
        l_ref[...] = jnp.zeros(l_ref.shape, F32)
        acc_ref[...] = jnp.zeros(acc_ref.shape, F32)

    half = tk // 2
    ones_rows = jnp.ones((BF16_SUBLANES, tk), BF16)

    def head_operands(h):
        return (k_ref[0, :, h * HEAD_W:(h + 1) * HEAD_W],
                qt_ref[0, h * HEAD_W:(h + 1) * HEAD_W, :])

    def scores_full(h):
        k, qt = head_operands(h)
        return (jnp.dot(k, qt, preferred_element_type=F32),)

    def softmax_full(h, st):
        m_prev = m_ref[h]
        m_new = jnp.maximum(m_prev, jnp.max(st, axis=0, keepdims=True))
        m_ref[h] = m_new
        return jnp.exp2(m_prev - m_new), jnp.exp2((st - m_new).astype(BF16))

    def softmax_bounded(h, *sts):
        return (None,) + tuple(jnp.exp2(st).astype(BF16) for st in sts)

    def values_full(h, vt_ones, e):
        return jnp.dot(vt_ones, e, preferred_element_type=F32)

    def scores_diag(h):
        k, qt = head_operands(h)
        tri = (lax.broadcasted_iota(jnp.int32, (half, half), 0)
               <= lax.broadcasted_iota(jnp.int32, (half, half), 1))
        st_lo = jnp.dot(k[:half], qt, preferred_element_type=F32)
        st_hi = jnp.dot(k[half:], qt[:, half:], preferred_element_type=F32)
        st_lo = jnp.concatenate([jnp.where(tri, st_lo[:, :half], -jnp.inf), st_lo[:, half:]],
                                axis=1)
        return st_lo, jnp.where(tri, st_hi, -jnp.inf)

    def softmax_diag(h, st_lo, st_hi):
        m_prev = m_ref[h]
        mx_lo = jnp.max(st_lo, axis=0, keepdims=True)
        mx_hi = jnp.max(st_hi, axis=0, keepdims=True)
        m_blk = jnp.concatenate([mx_lo[:, :half], jnp.maximum(mx_lo[:, half:], mx_hi)], axis=1)
        m_new = jnp.maximum(m_prev, m_blk)
        m_ref[h] = m_new
        return (jnp.exp2(m_prev - m_new), jnp.exp2((st_lo - m_new).astype(BF16)),
                jnp.exp2((st_hi - m_new[:, half:]).astype(BF16)))

    def values_diag(h, vt_ones, e_lo, e_hi):
        pv_lo = jnp.dot(vt_ones[:, :half], e_lo, preferred_element_type=F32)
        pv_hi = jnp.dot(vt_ones[:, half:], e_hi, preferred_element_type=F32)
        return jnp.concatenate([pv_lo[:, :half], pv_lo[:, half:] + pv_hi], axis=1)

    def step(scores, softmax, values):
        def accumulate(h, alpha, *e):
            rows = slice(h * V_HEAD_DIM, (h + 1) * V_HEAD_DIM)
            vt_ones = jnp.concatenate([vt_ref[0, rows, :], ones_rows], axis=0)
            pv = values(h, vt_ones, *e)
            if alpha is None:
                acc_ref[rows, :] += pv[:V_HEAD_DIM]
                l_ref[h] += pv[V_HEAD_DIM:V_HEAD_DIM + 1]
            else:
                acc_ref[rows, :] = alpha * acc_ref[rows, :] + pv[:V_HEAD_DIM]
                l_ref[h] = alpha * l_ref[h] + pv[V_HEAD_DIM:V_HEAD_DIM + 1]

        st = {0: scores(0), 1: scores(1)}
        pe = {0: softmax(0, *st.pop(0))}
        for h in range(N_HEADS):
            accumulate(h, *pe.pop(h))
            if h + 2 < N_HEADS:
                st[h + 2] = scores(h + 2)
            if h + 1 < N_HEADS:
                pe[h + 1] = softmax(h + 1, *st.pop(h + 1))

    @pl.when(jnp.logical_and(j < i, bounded))
    def _():
        step(scores_full, softmax_bounded, values_full)

    @pl.when(jnp.logical_and(j == i, bounded))
    def _():
        step(scores_diag, softmax_bounded, values_diag)

    @pl.when(jnp.logical_and(j < i, jnp.logical_not(bounded)))
    def _():
        step(scores_full, softmax_full, values_full)

    @pl.when(jnp.logical_and(j == i, jnp.logical_not(bounded)))
    def _():
        step(scores_diag, softmax_diag, values_diag)

    @pl.when(j == i)
    def _():
        o = jnp.concatenate(
            [(acc_ref[h * V_HEAD_DIM:(h + 1) * V_HEAD_DIM, :] / l_ref[h]).T.astype(BF16)
             for h in range(N_HEADS)], axis=-1)
        y = jnp.dot(o, wo_ref[...], preferred_element_type=F32)
        o_ref[0] = DEEPNORM_ALPHA * x_ref[0] + gate_ref[0] * y


def _score_bounded(q_norm2, k_norm2):
    q_max = jnp.sqrt(jnp.max(q_norm2, axis=-1))
    k_max = lax.cummax(jnp.sqrt(k_norm2[:, :, 0, :N_HEADS]), axis=1)
    return (jnp.max(q_max * k_max, axis=-1) <= SCORE_BOUND).astype(jnp.int32)


def _attn_layer(x, qt, k, vt, bounded, gate, w_out):
    b, s, d = x.shape
    t = min(s, ATTN_TILE)
    nq = s // t
    pairs = [(i, j) for i in range(nq) for j in range(i + 1)]
    qi_np = np.array([p[0] for p in pairs], np.int32)
    qi = jnp.asarray(qi_np)
    kj = jnp.asarray(np.array([p[1] for p in pairs], np.int32))
    bounded_steps = bounded[:, qi_np].reshape(-1)
    grid_spec = pltpu.PrefetchScalarGridSpec(
        num_scalar_prefetch=3,
        grid=(b, len(pairs)),
        in_specs=[
            pl.BlockSpec((1, N_HEADS * HEAD_W, t), lambda bi, p, qi, kj, bd: (bi, 0, qi[p])),
            pl.BlockSpec((1, t, N_HEADS * HEAD_W), lambda bi, p, qi, kj, bd: (bi, kj[p], 0)),
            pl.BlockSpec((1, N_HEADS * V_HEAD_DIM, t), lambda bi, p, qi, kj, bd: (bi, 0, kj[p])),
            pl.BlockSpec((1, t, d), lambda bi, p, qi, kj, bd: (bi, qi[p], 0)),
            pl.BlockSpec((1, 1, d), lambda bi, p, qi, kj, bd: (bi, 0, 0)),
            _const_spec((d, d)),
        ],
        out_specs=pl.BlockSpec((1, t, d), lambda bi, p, qi, kj, bd: (bi, qi[p], 0)),
        scratch_shapes=[
            pltpu.VMEM((N_HEADS, 1, t), F32),
            pltpu.VMEM((N_HEADS, 1, t), F32),
            pltpu.VMEM((N_HEADS * V_HEAD_DIM, t), F32),
        ],
    )
    return pl.pallas_call(
        _attn_kernel,
        out_shape=jax.ShapeDtypeStruct((b, s, d), F32),
        grid_spec=grid_spec,
        compiler_params=_params(("arbitrary", "arbitrary")),
        name="attn_layer",
    )(qi, kj, bounded_steps, qt, k, vt, x, gate, w_out)


def _swap_halves(w):
    half = w.shape[-1] // 2
    return jnp.concatenate([w[..., half:], w[..., :half]], axis=-1)


def _regroup_q_up(w):
    r = w.shape[0]
    w = w.reshape(r, N_HEADS, QK_NOPE_DIM + QK_ROPE_DIM)
    nope = w[..., :QK_NOPE_DIM].reshape(r, N_HEADS * QK_NOPE_DIM)
    rope = w[..., QK_NOPE_DIM:].reshape(r, N_HEADS * QK_ROPE_DIM)
    return jnp.concatenate([nope, rope], axis=-1).T.astype(BF16)


def kernel(x, c, positions, ada_w, ada_b, ln_g, ln_b, mlp_w1, mlp_w2, pool_w, pool_scale,
           q_down_w, q_norm_g, q_up_w, attn_out_w, kv_in_w, kv_norm_g, k_up_w, v_up_w):
    b, s, d = x.shape
    mod = _adaln_mod(c, ada_w, ada_b).reshape(DEPTH, b, N_MOD, 1, d)
    tab, tabt = _rope_table(positions)

    kv_w_in = jnp.concatenate([kv_in_w, _swap_halves(kv_in_w[:, KV_LORA_RANK:])],
                              axis=-1).astype(BF16)

    k = vt = k_n2 = None
    for l in range(DEPTH):
        shift1, scale1, gate1, shift2, scale2, gate2 = (mod[l, :, m] for m in range(N_MOD))
        g1, b1 = ln_g[l, 0].reshape(1, d), ln_b[l, 0].reshape(1, d)
        g2, b2 = ln_g[l, 1].reshape(1, d), ln_b[l, 1].reshape(1, d)
        if l < N_A_LAYERS:
            z = _pool_layer(x, shift1, scale1, gate1, pool_w[l].astype(BF16),
                            pool_scale[l].reshape(1, d))
        else:
            if l == N_A_LAYERS:
                k, vt, k_n2 = _kv_proj(x, tab, kv_w_in, kv_norm_g.reshape(1, KV_LORA_RANK),
                                       k_up_w.astype(BF16), v_up_w.T.astype(BF16))
            jl = l - N_A_LAYERS
            qt, q_n2 = _q_proj(x, tabt, shift1, scale1, q_down_w[jl].astype(BF16),
                               q_norm_g[jl].reshape(1, Q_LORA_RANK), _regroup_q_up(q_up_w[jl]))
            z = _attn_layer(x, qt, k, vt, _score_bounded(q_n2, k_n2), gate1,
                            attn_out_w[jl].astype(BF16))
        x = _mlp_layer(z, g1, b1, shift2, scale2, gate2, mlp_w1[l].astype(BF16),
                       mlp_w2[l].astype(BF16), g2, b2)
    return x
```

```python
import functools

import numpy as np
import jax
import jax.numpy as jnp
from jax import lax
from jax.experimental import pallas as pl
from jax.experimental.pallas import tpu as pltpu

D_MODEL = 1024
DEPTH = 4
N_A_LAYERS = DEPTH // 2
POOL_WINDOWS = (2, 4, 8, 16)
N_POOL_GROUPS = len(POOL_WINDOWS)
POOL_GROUP_DIM = D_MODEL // N_POOL_GROUPS
POOL_HALO = 16
QK_NOPE_DIM = 128
QK_ROPE_DIM = 64
V_HEAD_DIM = 128
N_HEADS = D_MODEL // 128
Q_LORA_RANK = D_MODEL // 2
KV_LORA_RANK = D_MODEL // 4
ROPE_THETA = 10000.0
D_FF = 4 * D_MODEL
N_MOD = 6
DEEPNORM_ALPHA = (2.0 * DEPTH) ** 0.25
LN_EPS = 1e-5
RMS_EPS = 1e-6
ATTN_SCALE = (QK_NOPE_DIM + QK_ROPE_DIM) ** -0.5
Q_SCALE = ATTN_SCALE * float(np.log2(np.e))

LANES = 128
BF16_SUBLANES = 16
ATTN_TILE = 512
SCORE_BOUND = 60.0
NORM_SLACK = 1.0 + 2.0 ** -6
HEAD_W = 2 * LANES
ROPE_HALF = QK_ROPE_DIM // 2

VMEM_LIMIT = 56 * 1024 * 1024

F32 = jnp.float32
BF16 = jnp.bfloat16

_NT_DIMS = (((1,), (1,)), ((), ()))


def _params(semantics):
    return pltpu.CompilerParams(dimension_semantics=semantics, vmem_limit_bytes=VMEM_LIMIT)


def _const_spec(shape):
    nd = len(shape)
    return pl.BlockSpec(shape, lambda *_: (0,) * nd, pipeline_mode=pl.Buffered(1))


def _layer_norm(z, g, b):
    mu = jnp.mean(z, axis=-1, keepdims=True)
    zc = z - mu
    var = jnp.mean(zc * zc, axis=-1, keepdims=True)
    return zc * lax.rsqrt(var + LN_EPS) * g + b


def _zero_after(v):
    bits = lax.bitcast_convert_type(v, jnp.uint32)
    bits = lax.shift_right_logical(lax.shift_right_logical(bits, jnp.uint32(16)), jnp.uint32(16))
    return lax.bitcast_convert_type(bits, F32)


def _rope_pair_sum(y):
    return y + pltpu.roll(y, QK_ROPE_DIM, axis=1)


def _adaln_kernel(c_ref, w_ref, b_ref, o_ref):
    c = c_ref[...]
    sc = c * jax.nn.sigmoid(c)
    o_ref[0] = jnp.dot(sc.astype(BF16), w_ref[0].astype(BF16),
                       preferred_element_type=F32) + b_ref[0]


def _adaln_mod(c, ada_w, ada_b):
    b, d = c.shape
    n = ada_w.shape[-1]
    tn = 1536
    return pl.pallas_call(
        _adaln_kernel,
        out_shape=jax.ShapeDtypeStruct((DEPTH, b, n), F32),
        grid=(DEPTH, n // tn),
        in_specs=[
            pl.BlockSpec((b, d), lambda l, j: (0, 0)),
            pl.BlockSpec((1, d, tn), lambda l, j: (l, 0, j)),
            pl.BlockSpec((1, 1, tn), lambda l, j: (l, 0, j)),
        ],
        out_specs=pl.BlockSpec((1, b, tn), lambda l, j: (l, 0, j)),
        compiler_params=_params(("arbitrary", "arbitrary")),
        name="adaln_mod",
    )(c, ada_w, ada_b.reshape(DEPTH, 1, n))


def _rope_table_kernel(pos_ref, freq_ref, o_ref, ot_ref):
    pos = pos_ref[0].astype(F32)
    ang = freq_ref[...] * pos
    cos = jnp.cos(ang)
    sin = jnp.sin(ang)
    tab = jnp.concatenate([cos, cos, -sin, sin], axis=0)
    ot_ref[0] = tab
    o_ref[0] = tab.T


def _rope_table(positions):
    b, s = positions.shape
    t = min(s, 1024)
    inv_freq = ROPE_THETA ** (-jnp.arange(0, QK_ROPE_DIM, 2, dtype=F32) / QK_ROPE_DIM)
    return pl.pallas_call(
        _rope_table_kernel,
        out_shape=(jax.ShapeDtypeStruct((b, s, LANES), F32),
                   jax.ShapeDtypeStruct((b, LANES, s), F32)),
        grid=(b, s // t),
        in_specs=[
            pl.BlockSpec((1, 1, t), lambda bi, i: (bi, 0, i)),
            pl.BlockSpec((ROPE_HALF, 1), lambda bi, i: (0, 0)),
        ],
        out_specs=(pl.BlockSpec((1, t, LANES), lambda bi, i: (bi, i, 0)),
                   pl.BlockSpec((1, LANES, t), lambda bi, i: (bi, 0, i))),
        compiler_params=_params(("arbitrary", "arbitrary")),
        name="rope_table",
    )(positions.reshape(b, 1, s), inv_freq.reshape(ROPE_HALF, 1))


def _after(v, after):
    return v if after is None else v + _zero_after(after[0:1, :])


def _block_kernel(*refs, pooled, ff_chunk, sub_rows):
    if pooled:
        (x_ref, xh_ref, shift1_ref, scale1_ref, gate1_ref, wp_ref, ps_ref, g1_ref, b1_ref,
         shift_ref, scale_ref, gate_ref, w1_ref, w2_ref, g2_ref, b2_ref,
         o_ref, x1_ref, h_ref, hm_ref) = refs
        in_ref = x_ref
    else:
        (in_ref, g1_ref, b1_ref, shift_ref, scale_ref, gate_ref, w1_ref, w2_ref, g2_ref, b2_ref,
         o_ref, x1_ref, h_ref) = refs
    t = in_ref.shape[1]
    n_sub = t // sub_rows
    n_chunks = D_FF // ff_chunk
    part_rows = sub_rows // n_chunks
    mul = 1.0 + scale_ref[0]
    shift = shift_ref[0]

    def sub(r):
        return slice(r * sub_rows, (r + 1) * sub_rows)

    def normalise(r, lo, hi, after):
        rows = slice(r * sub_rows + lo, r * sub_rows + hi)
        z = x1_ref[rows, :] if pooled else in_ref[0, rows, :]
        x1_ref[rows, :] = _layer_norm(z, _after(g1_ref[...], after), b1_ref[...])

    def modulate(r, lo, hi, after):
        rows = slice(r * sub_rows + lo, r * sub_rows + hi)
        h_ref[rows, :] = (x1_ref[rows, :] * _after(mul, after) + shift).astype(BF16)

    if pooled:
        i = pl.program_id(1)
        mul1 = 1.0 + scale1_ref[0]
        shift1 = shift1_ref[0]
        pending = {}

        def window_mean_minus_token(r, g, after=None):
            w = POOL_WINDOWS[g]
            cols = slice(g * POOL_GROUP_DIM, (g + 1) * POOL_GROUP_DIM)
            level = hm_ref[:, cols]
            if after is not None:
                level = level + _zero_after(after[0:1, cols])
            span = 1
            while span < w:
                level = level + pltpu.roll(level, span, axis=0)
                span *= 2
            row = i * t + r * sub_rows + lax.broadcasted_iota(jnp.int32, (sub_rows, 1), 0)
            cnt = jnp.minimum(row + 1, w).astype(F32)
            return (level[POOL_HALO:] / cnt - hm_ref[POOL_HALO:, cols]).astype(BF16)

        def pool_open(r, after):
            if r == 0:
                hist = jnp.where(i == 0, 0.0, xh_ref[0] * mul1 + shift1)
            else:
                hist = hm_ref[sub_rows:, :]
            hm_ref[:POOL_HALO, :] = hist
            hm_ref[POOL_HALO:, :] = x_ref[0, sub(r), :] * _after(mul1, after) + shift1
            pending[r] = [window_mean_minus_token(r, g) for g in range(2)]

        def pool_wide(r, after):
            pending[r] += [window_mean_minus_token(r, g, after) for g in range(2, N_POOL_GROUPS)]

        def pool_close(r, after):
            ys = pending.pop(r)
            mixed = jnp.concatenate(
                [jnp.dot(y, wp_ref[g], preferred_element_type=F32) for g, y in enumerate(ys)],
                axis=-1) * _after(ps_ref[...], after)
            x1_ref[sub(r), :] = DEEPNORM_ALPHA * x_ref[0, sub(r), :] + gate1_ref[0] * mixed

        def norm_mod(r, after):
            normalise(r, 0, sub_rows, after)
            modulate(r, 0, sub_rows, None)

        stages = [pool_open, pool_wide, pool_close, norm_mod]
        assert len(stages) <= n_chunks

        def enter(r, c, after):
            if c < len(stages):
                stages[c](r, after)
    else:
        def enter(r, c, after):
            normalise(r, c * part_rows, (c + 1) * part_rows, after)
            modulate(r, c * part_rows, (c + 1) * part_rows, after)

    def ff_part(h, c):
        cols = slice(c * ff_chunk, (c + 1) * ff_chunk)
        a = jnp.maximum(jnp.dot(h, w1_ref[:, cols], preferred_element_type=F32), 0.0)
        return jnp.dot((a * a).astype(BF16), w2_ref[cols, :], preferred_element_type=F32)

    def finish(r, y, lo, hi, after=None):
        rows = slice(r * sub_rows + lo, r * sub_rows + hi)
        z2 = DEEPNORM_ALPHA * x1_ref[rows, :] + _after(gate_ref[0], after) * y[lo:hi]
        o_ref[0, rows, :] = _layer_norm(z2, g2_ref[...], b2_ref[...])

    for c in range(n_chunks):
        enter(0, c, None)
    prev = None
    for r in range(n_sub):
        h = h_ref[sub(r), :]
        y = None
        for c in range(n_chunks):
            part = ff_part(h, c)
            y = part if y is None else y + part
            if r + 1 < n_sub:
                enter(r + 1, c, part)
            if prev is not None:
                finish(r - 1, prev, c * part_rows, (c + 1) * part_rows, after=part)
        prev = y
    finish(n_sub - 1, prev, 0, sub_rows)


def _block_layer(xz, pool_args, ln1_g, ln1_b, shift, scale, gate, w1, w2, ln2_g, ln2_b):
    b, s, d = xz.shape
    t = min(s, 1024)
    sub_rows = min(t, 256)
    mod_spec = pl.BlockSpec((1, 1, d), lambda bi, i: (bi, 0, 0))
    row_spec = pl.BlockSpec((1, t, d), lambda bi, i: (bi, i, 0))
    tail_specs = [
        _const_spec((1, d)), _const_spec((1, d)),
        mod_spec, mod_spec, mod_spec,
        _const_spec((d, D_FF)), _const_spec((D_FF, d)),
        _const_spec((1, d)), _const_spec((1, d)),
    ]
    tail_args = (ln1_g, ln1_b, shift, scale, gate, w1, w2, ln2_g, ln2_b)
    scratch = [pltpu.VMEM((t, d), F32), pltpu.VMEM((t, d), BF16)]
    if pool_args is None:
        in_specs, args = [row_spec] + tail_specs, (xz,) + tail_args
    else:
        halo_blocks = t // POOL_HALO
        in_specs = [
            row_spec,
            pl.BlockSpec((1, POOL_HALO, d),
                         lambda bi, i: (bi, jnp.maximum(i * halo_blocks - 1, 0), 0)),
            mod_spec, mod_spec, mod_spec,
            _const_spec((N_POOL_GROUPS, POOL_GROUP_DIM, POOL_GROUP_DIM)),
            _const_spec((1, d)),
        ] + tail_specs
        args = (xz, xz) + tuple(pool_args) + tail_args
        scratch = scratch + [pltpu.VMEM((POOL_HALO + sub_rows, d), F32)]
    return pl.pallas_call(
        functools.partial(_block_kernel, pooled=pool_args is not None, ff_chunk=1024,
                          sub_rows=sub_rows),
        out_shape=jax.ShapeDtypeStruct((b, s, d), F32),
        grid=(b, s // t),
        in_specs=in_specs,
        out_specs=row_spec,
        scratch_shapes=scratch,
        compiler_params=_params(("arbitrary", "arbitrary")),
        name="pool_mlp_layer" if pool_args is not None else "mlp_layer",
    )(*args)


def _kv_kernel(x_ref, tab_ref, win_ref, ng_ref, kup_ref, vup_ref, ind_ref, k_ref, vt_ref, n2_ref):
    xb = x_ref[0].astype(BF16)
    cy = jnp.dot(xb, win_ref[...], preferred_element_type=F32)
    c = cy[:, :KV_LORA_RANK]
    ms = jnp.mean(c * c, axis=-1, keepdims=True)
    ckv = (c * lax.rsqrt(ms + RMS_EPS) * ng_ref[...]).astype(BF16)
    y = cy[:, KV_LORA_RANK:] * tab_ref[0]
    lane = lax.broadcasted_iota(jnp.int32, y.shape, 1)
    kr = jnp.where(lane < QK_ROPE_DIM, _rope_pair_sum(y), 0.0)
    kn = jnp.dot(ckv, kup_ref[...], preferred_element_type=F32)
    kr_bf = kr.astype(BF16)
    for h in range(N_HEADS):
        k_ref[0, :, h * HEAD_W:h * HEAD_W + LANES] = kn[:, h * LANES:(h + 1) * LANES].astype(BF16)
        k_ref[0, :, h * HEAD_W + LANES:(h + 1) * HEAD_W] = kr_bf
    nope_n2 = jnp.dot((kn * kn).astype(BF16), ind_ref[...], preferred_element_type=F32)
    key_n2 = (nope_n2 + jnp.sum(kr * kr, axis=1, keepdims=True)) * NORM_SLACK
    n2_ref[0, 0] = jnp.max(key_n2, axis=0, keepdims=True)
    vt_ref[0] = lax.dot_general(vup_ref[...], ckv, _NT_DIMS,
                                preferred_element_type=F32).astype(BF16)


def _kv_proj(x, tab, w_in, norm_g, k_up, v_up_t):
    b, s, d = x.shape
    t = min(s, ATTN_TILE)
    head_of_lane = np.arange(N_HEADS * QK_NOPE_DIM) // QK_NOPE_DIM
    indicator = jnp.asarray(head_of_lane[:, None] == np.arange(LANES)[None, :], BF16)
    return pl.pallas_call(
        _kv_kernel,
        out_shape=(jax.ShapeDtypeStruct((b, s, N_HEADS * HEAD_W), BF16),
                   jax.ShapeDtypeStruct((b, N_HEADS * V_HEAD_DIM, s), BF16),
                   jax.ShapeDtypeStruct((b, s // t, 1, LANES), F32)),
        grid=(b, s // t),
        in_specs=[
            pl.BlockSpec((1, t, d), lambda bi, i: (bi, i, 0)),
            pl.BlockSpec((1, t, LANES), lambda bi, i: (bi, i, 0)),
            _const_spec((d, KV_LORA_RANK + LANES)),
            _const_spec((1, KV_LORA_RANK)),
            _const_spec((KV_LORA_RANK, N_HEADS * QK_NOPE_DIM)),
            _const_spec((N_HEADS * V_HEAD_DIM, KV_LORA_RANK)),
            _const_spec((N_HEADS * QK_NOPE_DIM, LANES)),
        ],
        out_specs=(pl.BlockSpec((1, t, N_HEADS * HEAD_W), lambda bi, i: (bi, i, 0)),
                   pl.BlockSpec((1, N_HEADS * V_HEAD_DIM, t), lambda bi, i: (bi, 0, i)),
                   pl.BlockSpec((1, 1, 1, LANES), lambda bi, i: (bi, i, 0, 0))),
        compiler_params=_params(("arbitrary", "arbitrary")),
        name="kv_proj",
    )(x, tab, w_in, norm_g, k_up, v_up_t, indicator)


def _q_kernel(x_ref, tabt_ref, shift_ref, scale_ref, wd_ref, ng_ref, wq_ref, qt_ref, n2_ref):
    h = (x_ref[0] * (1.0 + scale_ref[0]) + shift_ref[0]).astype(BF16)
    c = jnp.dot(h, wd_ref[...], preferred_element_type=F32)
    ms = jnp.mean(c * c, axis=-1, keepdims=True)
    cq = (c * lax.rsqrt(ms + RMS_EPS) * ng_ref[...]).astype(BF16)
    qa = lax.dot_general(wq_ref[...], cq, _NT_DIMS, preferred_element_type=F32)
    tabt = tabt_ref[0]
    cos2, sin2 = tabt[:QK_ROPE_DIM], tabt[QK_ROPE_DIM:]
    t = tabt.shape[1]
    rope0 = N_HEADS * QK_NOPE_DIM
    n2 = []
    for hd in range(N_HEADS):
        nope = qa[hd * LANES:(hd + 1) * LANES] * Q_SCALE
        qt_ref[0, hd * HEAD_W:hd * HEAD_W + LANES, :] = nope.astype(BF16)
        a = qa[rope0 + hd * QK_ROPE_DIM:rope0 + (hd + 1) * QK_ROPE_DIM]
        swapped = jnp.concatenate([a[ROPE_HALF:], a[:ROPE_HALF]], axis=0)
        rot = (a * cos2 + swapped * sin2) * Q_SCALE
        qt_ref[0, hd * HEAD_W + LANES:hd * HEAD_W + LANES + QK_ROPE_DIM, :] = rot.astype(BF16)
        qt_ref[0, hd * HEAD_W + LANES + QK_ROPE_DIM:(hd + 1) * HEAD_W, :] = rot.astype(BF16)
        query_n2 = (jnp.sum(nope * nope, axis=0, keepdims=True)
                    + jnp.sum(rot * rot, axis=0, keepdims=True))
        folded = query_n2[:, :LANES]
        for c in range(1, t // LANES):
            folded = jnp.maximum(folded, query_n2[:, c * LANES:(c + 1) * LANES])
        n2.append(folded)
    n2_ref[0, 0] = jnp.concatenate(n2, axis=0)


def _q_proj(x, tabt, shift, scale, w_down, norm_g, w_q_t):
    b, s, d = x.shape
    t = min(s, ATTN_TILE)
    mod_spec = pl.BlockSpec((1, 1, d), lambda bi, i: (bi, 0, 0))
    return pl.pallas_call(
        _q_kernel,
        out_shape=(jax.ShapeDtypeStruct((b, N_HEADS * HEAD_W, s), BF16),
                   jax.ShapeDtypeStruct((b, s // t, N_HEADS, LANES), F32)),
        grid=(b, s // t),
        in_specs=[
            pl.BlockSpec((1, t, d), lambda bi, i: (bi, i, 0)),
            pl.BlockSpec((1, LANES, t), lambda bi, i: (bi, 0, i)),
            mod_spec, mod_spec,
            _const_spec((d, Q_LORA_RANK)), _const_spec((1, Q_LORA_RANK)),
            _const_spec((N_HEADS * (QK_NOPE_DIM + QK_ROPE_DIM), Q_LORA_RANK)),
        ],
        out_specs=(pl.BlockSpec((1, N_HEADS * HEAD_W, t), lambda bi, i: (bi, 0, i)),
                   pl.BlockSpec((1, 1, N_HEADS, LANES), lambda bi, i: (bi, i, 0, 0))),
        compiler_params=_params(("arbitrary", "arbitrary")),
        name="q_proj",
    )(x, tabt, shift, scale, w_down, norm_g, w_q_t)


def _attn_kernel(qi_ref, kj_ref, bounded_ref, qt_ref, k_ref, vt_ref, x_ref, gate_ref, wo_ref,
                 o_ref, m_ref, l_ref, acc_ref):
    p = pl.program_id(1)
    i = qi_ref[p]
    j = kj_ref[p]
    bounded = bounded_ref[pl.program_id(0) * pl.num_programs(1) + p] != 0
    tk = k_ref.shape[1]

    @pl.when(j == 0)
    def _():
        m_ref[...] = jnp.full(m_ref.shape, -jnp.inf, F32)
        l_ref[...] = jnp.zeros(l_ref.shape, F32)
        acc_ref[...] = jnp.zeros(acc_ref.shape, F32)

    half = tk // 2
    ones_rows = jnp.ones((BF16_SUBLANES, tk), BF16)

    def head_operands(h):
        return (k_ref[0, :, h * HEAD_W:(h + 1) * HEAD_W],
                qt_ref[0, h * HEAD_W:(h + 1) * HEAD_W, :])

    def scores_full(h):
        k, qt = head_operands(h)
        return (jnp.dot(k, qt, preferred_element_type=F32),)

    def softmax_full(h, st):
        m_prev = m_ref[h]
        m_new = jnp.maximum(m_prev, jnp.max(st, axis=0, keepdims=True))
        m_ref[h] = m_new
        return jnp.exp2(m_prev - m_new), jnp.exp2((st - m_new).astype(BF16))

    def softmax_bounded(h, *sts):
        return (None,) + tuple(jnp.exp2(st).astype(BF16) for st in sts)

    def values_full(h, vt_ones, e):
        return jnp.dot(vt_ones, e, preferred_element_type=F32)

    def scores_diag(h):
        k, qt = head_operands(h)
        tri = (lax.broadcasted_iota(jnp.int32, (half, half), 0)
               <= lax.broadcasted_iota(jnp.int32, (half, half), 1))
        st_lo = jnp.dot(k[:half], qt, preferred_element_type=F32)
        st_hi = jnp.dot(k[half:], qt[:, half:], preferred_element_type=F32)
        st_lo = jnp.concatenate([jnp.where(tri, st_lo[:, :half], -jnp.inf), st_lo[:, half:]],
                                axis=1)
        return st_lo, jnp.where(tri, st_hi, -jnp.inf)

    def softmax_diag(h, st_lo, st_hi):
        m_prev = m_ref[h]
        mx_lo = jnp.max(st_lo, axis=0, keepdims=True)
        mx_hi = jnp.max(st_hi, axis=0, keepdims=True)
        m_blk = jnp.concatenate([mx_lo[:, :half], jnp.maximum(mx_lo[:, half:], mx_hi)], axis=1)
        m_new = jnp.maximum(m_prev, m_blk)
        m_ref[h] = m_new
        return (jnp.exp2(m_prev - m_new), jnp.exp2((st_lo - m_new).astype(BF16)),
                jnp.exp2((st_hi - m_new[:, half:]).astype(BF16)))

    def values_diag(h, vt_ones, e_lo, e_hi):
        pv_lo = jnp.dot(vt_ones[:, :half], e_lo, preferred_element_type=F32)
        pv_hi = jnp.dot(vt_ones[:, half:], e_hi, preferred_element_type=F32)
        return jnp.concatenate([pv_lo[:, :half], pv_lo[:, half:] + pv_hi], axis=1)

    def step(scores, softmax, values):
        def accumulate(h, alpha, *e):
            rows = slice(h * V_HEAD_DIM, (h + 1) * V_HEAD_DIM)
            vt_ones = jnp.concatenate([vt_ref[0, rows, :], ones_rows], axis=0)
            pv = values(h, vt_ones, *e)
            if alpha is None:
                acc_ref[rows, :] += pv[:V_HEAD_DIM]
                l_ref[h] += pv[V_HEAD_DIM:V_HEAD_DIM + 1]
            else:
                acc_ref[rows, :] = alpha * acc_ref[rows, :] + pv[:V_HEAD_DIM]
                l_ref[h] = alpha * l_ref[h] + pv[V_HEAD_DIM:V_HEAD_DIM + 1]

        st = {0: scores(0), 1: scores(1)}
        pe = {0: softmax(0, *st.pop(0))}
        for h in range(N_HEADS):
            accumulate(h, *pe.pop(h))
            if h + 2 < N_HEADS:
                st[h + 2] = scores(h + 2)
            if h + 1 < N_HEADS:
                pe[h + 1] = softmax(h + 1, *st.pop(h + 1))

    @pl.when(jnp.logical_and(j < i, bounded))
    def _():
        step(scores_full, softmax_bounded, values_full)

    @pl.when(jnp.logical_and(j == i, bounded))
    def _():
        step(scores_diag, softmax_bounded, values_diag)

    @pl.when(jnp.logical_and(j < i, jnp.logical_not(bounded)))
    def _():
        step(scores_full, softmax_full, values_full)

    @pl.when(jnp.logical_and(j == i, jnp.logical_not(bounded)))
    def _():
        step(scores_diag, softmax_diag, values_diag)

    @pl.when(j == i)
    def _():
        o = jnp.concatenate(
            [(acc_ref[h * V_HEAD_DIM:(h + 1) * V_HEAD_DIM, :] / l_ref[h]).T.astype(BF16)
             for h in range(N_HEADS)], axis=-1)
        y = jnp.dot(o, wo_ref[...], preferred_element_type=F32)
        o_ref[0] = DEEPNORM_ALPHA * x_ref[0] + gate_ref[0] * y


def _score_bounded(q_norm2, k_norm2):
    q_max = jnp.sqrt(jnp.max(q_norm2, axis=-1))
    k_max = lax.cummax(jnp.sqrt(k_norm2[:, :, 0, :N_HEADS]), axis=1)
    return (jnp.max(q_max * k_max, axis=-1) <= SCORE_BOUND).astype(jnp.int32)


def _attn_layer(x, qt, k, vt, bounded, gate, w_out):
    b, s, d = x.shape
    t = min(s, ATTN_TILE)
    nq = s // t
    pairs = [(i, j) for i in range(nq) for j in range(i + 1)]
    qi_np = np.array([p[0] for p in pairs], np.int32)
    qi = jnp.asarray(qi_np)
    kj = jnp.asarray(np.array([p[1] for p in pairs], np.int32))
    bounded_steps = bounded[:, qi_np].reshape(-1)
    grid_spec = pltpu.PrefetchScalarGridSpec(
        num_scalar_prefetch=3,
        grid=(b, len(pairs)),
        in_specs=[
            pl.BlockSpec((1, N_HEADS * HEAD_W, t), lambda bi, p, qi, kj, bd: (bi, 0, qi[p])),
            pl.BlockSpec((1, t, N_HEADS * HEAD_W), lambda bi, p, qi, kj, bd: (bi, kj[p], 0)),
            pl.BlockSpec((1, N_HEADS * V_HEAD_DIM, t), lambda bi, p, qi, kj, bd: (bi, 0, kj[p])),
            pl.BlockSpec((1, t, d), lambda bi, p, qi, kj, bd: (bi, qi[p], 0)),
            pl.BlockSpec((1, 1, d), lambda bi, p, qi, kj, bd: (bi, 0, 0)),
            _const_spec((d, d)),
        ],
        out_specs=pl.BlockSpec((1, t, d), lambda bi, p, qi, kj, bd: (bi, qi[p], 0)),
        scratch_shapes=[
            pltpu.VMEM((N_HEADS, 1, t), F32),
            pltpu.VMEM((N_HEADS, 1, t), F32),
            pltpu.VMEM((N_HEADS * V_HEAD_DIM, t), F32),
        ],
    )
    return pl.pallas_call(
        _attn_kernel,
        out_shape=jax.ShapeDtypeStruct((b, s, d), F32),
        grid_spec=grid_spec,
        compiler_params=_params(("arbitrary", "arbitrary")),
        name="attn_layer",
    )(qi, kj, bounded_steps, qt, k, vt, x, gate, w_out)


def _swap_halves(w):
    half = w.shape[-1] // 2
    return jnp.concatenate([w[..., half:], w[..., :half]], axis=-1)


def _regroup_q_up(w):
    r = w.shape[0]
    w = w.reshape(r, N_HEADS, QK_NOPE_DIM + QK_ROPE_DIM)
    nope = w[..., :QK_NOPE_DIM].reshape(r, N_HEADS * QK_NOPE_DIM)
    rope = w[..., QK_NOPE_DIM:].reshape(r, N_HEADS * QK_ROPE_DIM)
    return jnp.concatenate([nope, rope], axis=-1).T.astype(BF16)


def kernel(x, c, positions, ada_w, ada_b, ln_g, ln_b, mlp_w1, mlp_w2, pool_w, pool_scale,
           q_down_w, q_norm_g, q_up_w, attn_out_w, kv_in_w, kv_norm_g, k_up_w, v_up_w):
    b, s, d = x.shape
    mod = _adaln_mod(c, ada_w, ada_b).reshape(DEPTH, b, N_MOD, 1, d)
    tab, tabt = _rope_table(positions)

    kv_w_in = jnp.concatenate([kv_in_w, _swap_halves(kv_in_w[:, KV_LORA_RANK:])],
                              axis=-1).astype(BF16)

    k = vt = k_n2 = None
    for l in range(DEPTH):
        shift1, scale1, gate1, shift2, scale2, gate2 = (mod[l, :, m] for m in range(N_MOD))
        g1, b1 = ln_g[l, 0].reshape(1, d), ln_b[l, 0].reshape(1, d)
        g2, b2 = ln_g[l, 1].reshape(1, d), ln_b[l, 1].reshape(1, d)
        if l < N_A_LAYERS:
            xz = x
            pool_args = (shift1, scale1, gate1, pool_w[l].astype(BF16),
                         pool_scale[l].reshape(1, d))
        else:
            if l == N_A_LAYERS:
                k, vt, k_n2 = _kv_proj(x, tab, kv_w_in, kv_norm_g.reshape(1, KV_LORA_RANK),
                                       k_up_w.astype(BF16), v_up_w.T.astype(BF16))
            jl = l - N_A_LAYERS
            qt, q_n2 = _q_proj(x, tabt, shift1, scale1, q_down_w[jl].astype(BF16),
                               q_norm_g[jl].reshape(1, Q_LORA_RANK), _regroup_q_up(q_up_w[jl]))
            xz = _attn_layer(x, qt, k, vt, _score_bounded(q_n2, k_n2), gate1,
                             attn_out_w[jl].astype(BF16))
            pool_args = None
        x = _block_layer(xz, pool_args, g1, b1, shift2, scale2, gate2, mlp_w1[l].astype(BF16),
                         mlp_w2[l].astype(BF16), g2, b2)
    return x
```

```python
import functools

import numpy as np
import jax
import jax.numpy as jnp
from jax import lax
from jax.experimental import pallas as pl
from jax.experimental.pallas import tpu as pltpu

D_MODEL = 1024
DEPTH = 4
N_A_LAYERS = DEPTH // 2
POOL_WINDOWS = (2, 4, 8, 16)
N_POOL_GROUPS = len(POOL_WINDOWS)
POOL_GROUP_DIM = D_MODEL // N_POOL_GROUPS
POOL_HALO = 16
QK_NOPE_DIM = 128
QK_ROPE_DIM = 64
V_HEAD_DIM = 128
N_HEADS = D_MODEL // 128
Q_LORA_RANK = D_MODEL // 2
KV_LORA_RANK = D_MODEL // 4
ROPE_THETA = 10000.0
D_FF = 4 * D_MODEL
N_MOD = 6
DEEPNORM_ALPHA = (2.0 * DEPTH) ** 0.25
LN_EPS = 1e-5
RMS_EPS = 1e-6
ATTN_SCALE = (QK_NOPE_DIM + QK_ROPE_DIM) ** -0.5
Q_SCALE = ATTN_SCALE * float(np.log2(np.e))

LANES = 128
BF16_SUBLANES = 16
ATTN_TILE = 1024
ATTN_KEY_TILE = 512
SCORE_BOUND = 60.0
NORM_SLACK = 1.0 + 2.0 ** -6
HEAD_W = 2 * LANES
Q_ROWS = N_HEADS * (QK_NOPE_DIM + QK_ROPE_DIM)
ROPE_HALF = QK_ROPE_DIM // 2

VMEM_LIMIT = 56 * 1024 * 1024

F32 = jnp.float32
BF16 = jnp.bfloat16

_NT_DIMS = (((1,), (1,)), ((), ()))


def _params(semantics):
    return pltpu.CompilerParams(dimension_semantics=semantics, vmem_limit_bytes=VMEM_LIMIT)


def _const_spec(shape):
    nd = len(shape)
    return pl.BlockSpec(shape, lambda *_: (0,) * nd, pipeline_mode=pl.Buffered(1))


def _layer_norm(z, g, b):
    mu = jnp.mean(z, axis=-1, keepdims=True)
    zc = z - mu
    var = jnp.mean(zc * zc, axis=-1, keepdims=True)
    return zc * lax.rsqrt(var + LN_EPS) * g + b


def _zero_after(v):
    bits = lax.bitcast_convert_type(v, jnp.uint32)
    bits = lax.shift_right_logical(lax.shift_right_logical(bits, jnp.uint32(16)), jnp.uint32(16))
    return lax.bitcast_convert_type(bits, F32)


def _rope_pair_sum(y):
    return y + pltpu.roll(y, QK_ROPE_DIM, axis=1)


def _adaln_kernel(c_ref, w_ref, b_ref, o_ref):
    c = c_ref[...]
    sc = c * jax.nn.sigmoid(c)
    o_ref[0] = jnp.dot(sc.astype(BF16), w_ref[0].astype(BF16),
                       preferred_element_type=F32) + b_ref[0]


def _adaln_mod(c, ada_w, ada_b):
    b, d = c.shape
    n = ada_w.shape[-1]
    tn = 1536
    return pl.pallas_call(
        _adaln_kernel,
        out_shape=jax.ShapeDtypeStruct((DEPTH, b, n), F32),
        grid=(DEPTH, n // tn),
        in_specs=[
            pl.BlockSpec((b, d), lambda l, j: (0, 0)),
            pl.BlockSpec((1, d, tn), lambda l, j: (l, 0, j)),
            pl.BlockSpec((1, 1, tn), lambda l, j: (l, 0, j)),
        ],
        out_specs=pl.BlockSpec((1, b, tn), lambda l, j: (l, 0, j)),
        compiler_params=_params(("arbitrary", "arbitrary")),
        name="adaln_mod",
    )(c, ada_w, ada_b.reshape(DEPTH, 1, n))


def _rope_table_kernel(pos_ref, freq_ref, o_ref, ot_ref):
    pos = pos_ref[0].astype(F32)
    ang = freq_ref[...] * pos
    cos = jnp.cos(ang)
    sin = jnp.sin(ang)
    tab = jnp.concatenate([cos, cos, -sin, sin], axis=0)
    ot_ref[0] = tab
    o_ref[0] = tab.T


def _rope_table(positions):
    b, s = positions.shape
    t = min(s, 1024)
    inv_freq = ROPE_THETA ** (-jnp.arange(0, QK_ROPE_DIM, 2, dtype=F32) / QK_ROPE_DIM)
    return pl.pallas_call(
        _rope_table_kernel,
        out_shape=(jax.ShapeDtypeStruct((b, s, LANES), F32),
                   jax.ShapeDtypeStruct((b, LANES, s), F32)),
        grid=(b, s // t),
        in_specs=[
            pl.BlockSpec((1, 1, t), lambda bi, i: (bi, 0, i)),
            pl.BlockSpec((ROPE_HALF, 1), lambda bi, i: (0, 0)),
        ],
        out_specs=(pl.BlockSpec((1, t, LANES), lambda bi, i: (bi, i, 0)),
                   pl.BlockSpec((1, LANES, t), lambda bi, i: (bi, 0, i))),
        compiler_params=_params(("arbitrary", "arbitrary")),
        name="rope_table",
    )(positions.reshape(b, 1, s), inv_freq.reshape(ROPE_HALF, 1))


def _after(v, after):
    return v if after is None else v + _zero_after(after[0:1, :])


def _block_kernel(*refs, pooled, ff_chunk, sub_rows):
    if pooled:
        (x_ref, xh_ref, shift1_ref, scale1_ref, gate1_ref, wp_ref, ps_ref, g1_ref, b1_ref,
         shift_ref, scale_ref, gate_ref, w1_ref, w2_ref, g2_ref, b2_ref,
         o_ref, x1_ref, h_ref, hm_ref) = refs
        in_ref = x_ref
    else:
        (in_ref, g1_ref, b1_ref, shift_ref, scale_ref, gate_ref, w1_ref, w2_ref, g2_ref, b2_ref,
         o_ref, x1_ref, h_ref) = refs
    t = in_ref.shape[1]
    n_sub = t // sub_rows
    n_chunks = D_FF // ff_chunk
    part_rows = sub_rows // n_chunks
    mul = 1.0 + scale_ref[0]
    shift = shift_ref[0]

    def sub(r):
        return slice(r * sub_rows, (r + 1) * sub_rows)

    def normalise(r, lo, hi, after):
        rows = slice(r * sub_rows + lo, r * sub_rows + hi)
        z = x1_ref[rows, :] if pooled else in_ref[0, rows, :]
        x1_ref[rows, :] = _layer_norm(z, _after(g1_ref[...], after), b1_ref[...])

    def modulate(r, lo, hi, after):
        rows = slice(r * sub_rows + lo, r * sub_rows + hi)
        h_ref[rows, :] = (x1_ref[rows, :] * _after(mul, after) + shift).astype(BF16)

    if pooled:
        i = pl.program_id(1)
        mul1 = 1.0 + scale1_ref[0]
        shift1 = shift1_ref[0]
        pending = {}

        def window_mean_minus_token(r, g, after=None):
            w = POOL_WINDOWS[g]
            cols = slice(g * POOL_GROUP_DIM, (g + 1) * POOL_GROUP_DIM)
            level = hm_ref[:, cols]
            if after is not None:
                level = level + _zero_after(after[0:1, cols])
            span = 1
            while span < w:
                level = level + pltpu.roll(level, span, axis=0)
                span *= 2
            row = i * t + r * sub_rows + lax.broadcasted_iota(jnp.int32, (sub_rows, 1), 0)
            cnt = jnp.minimum(row + 1, w).astype(F32)
            return (level[POOL_HALO:] / cnt - hm_ref[POOL_HALO:, cols]).astype(BF16)

        def pool_open(r, after):
            if r == 0:
                hist = jnp.where(i == 0, 0.0, xh_ref[0] * mul1 + shift1)
            else:
                hist = hm_ref[sub_rows:, :]
            hm_ref[:POOL_HALO, :] = hist
            hm_ref[POOL_HALO:, :] = x_ref[0, sub(r), :] * _after(mul1, after) + shift1
            pending[r] = [window_mean_minus_token(r, g) for g in range(2)]

        def pool_wide(r, after):
            pending[r] += [window_mean_minus_token(r, g, after) for g in range(2, N_POOL_GROUPS)]

        def pool_close(r, after):
            ys = pending.pop(r)
            mixed = jnp.concatenate(
                [jnp.dot(y, wp_ref[g], preferred_element_type=F32) for g, y in enumerate(ys)],
                axis=-1) * _after(ps_ref[...], after)
            x1_ref[sub(r), :] = DEEPNORM_ALPHA * x_ref[0, sub(r), :] + gate1_ref[0] * mixed

        def norm_mod(r, after):
            normalise(r, 0, sub_rows, after)
            modulate(r, 0, sub_rows, None)

        stages = [pool_open, pool_wide, pool_close, norm_mod]
        assert len(stages) <= n_chunks

        def enter(r, c, after):
            if c < len(stages):
                stages[c](r, after)
    else:
        def enter(r, c, after):
            normalise(r, c * part_rows, (c + 1) * part_rows, after)
            modulate(r, c * part_rows, (c + 1) * part_rows, after)

    def ff_part(h, c):
        cols = slice(c * ff_chunk, (c + 1) * ff_chunk)
        a = jnp.maximum(jnp.dot(h, w1_ref[:, cols], preferred_element_type=F32), 0.0)
        return jnp.dot((a * a).astype(BF16), w2_ref[cols, :], preferred_element_type=F32)

    def finish(r, y, lo, hi, after=None):
        rows = slice(r * sub_rows + lo, r * sub_rows + hi)
        z2 = DEEPNORM_ALPHA * x1_ref[rows, :] + _after(gate_ref[0], after) * y[lo:hi]
        o_ref[0, rows, :] = _layer_norm(z2, g2_ref[...], b2_ref[...])

    for c in range(n_chunks):
        enter(0, c, None)
    prev = None
    for r in range(n_sub):
        h = h_ref[sub(r), :]
        y = None
        for c in range(n_chunks):
            part = ff_part(h, c)
            y = part if y is None else y + part
            if r + 1 < n_sub:
                enter(r + 1, c, part)
            if prev is not None:
                finish(r - 1, prev, c * part_rows, (c + 1) * part_rows, after=part)
        prev = y
    finish(n_sub - 1, prev, 0, sub_rows)


def _block_layer(xz, pool_args, ln1_g, ln1_b, shift, scale, gate, w1, w2, ln2_g, ln2_b):
    b, s, d = xz.shape
    t = min(s, 1024)
    sub_rows = min(t, 256)
    mod_spec = pl.BlockSpec((1, 1, d), lambda bi, i: (bi, 0, 0))
    row_spec = pl.BlockSpec((1, t, d), lambda bi, i: (bi, i, 0))
    tail_specs = [
        _const_spec((1, d)), _const_spec((1, d)),
        mod_spec, mod_spec, mod_spec,
        _const_spec((d, D_FF)), _const_spec((D_FF, d)),
        _const_spec((1, d)), _const_spec((1, d)),
    ]
    tail_args = (ln1_g, ln1_b, shift, scale, gate, w1, w2, ln2_g, ln2_b)
    scratch = [pltpu.VMEM((t, d), F32), pltpu.VMEM((t, d), BF16)]
    if pool_args is None:
        in_specs, args = [row_spec] + tail_specs, (xz,) + tail_args
    else:
        halo_blocks = t // POOL_HALO
        in_specs = [
            row_spec,
            pl.BlockSpec((1, POOL_HALO, d),
                         lambda bi, i: (bi, jnp.maximum(i * halo_blocks - 1, 0), 0)),
            mod_spec, mod_spec, mod_spec,
            _const_spec((N_POOL_GROUPS, POOL_GROUP_DIM, POOL_GROUP_DIM)),
            _const_spec((1, d)),
        ] + tail_specs
        args = (xz, xz) + tuple(pool_args) + tail_args
        scratch = scratch + [pltpu.VMEM((POOL_HALO + sub_rows, d), F32)]
    return pl.pallas_call(
        functools.partial(_block_kernel, pooled=pool_args is not None, ff_chunk=1024,
                          sub_rows=sub_rows),
        out_shape=jax.ShapeDtypeStruct((b, s, d), F32),
        grid=(b, s // t),
        in_specs=in_specs,
        out_specs=row_spec,
        scratch_shapes=scratch,
        compiler_params=_params(("arbitrary", "arbitrary")),
        name="pool_mlp_layer" if pool_args is not None else "mlp_layer",
    )(*args)


def _kv_kernel(x_ref, tab_ref, win_ref, ng_ref, kup_ref, vup_ref, ind_ref,
               kn_ref, kr_ref, vt_ref, n2_ref):
    xb = x_ref[0].astype(BF16)
    cy = jnp.dot(xb, win_ref[...], preferred_element_type=F32)
    c = cy[:, :KV_LORA_RANK]
    ms = jnp.mean(c * c, axis=-1, keepdims=True)
    ckv = (c * lax.rsqrt(ms + RMS_EPS) * ng_ref[...]).astype(BF16)
    y = cy[:, KV_LORA_RANK:] * tab_ref[0]
    lane = lax.broadcasted_iota(jnp.int32, y.shape, 1)
    kr = jnp.where(lane < QK_ROPE_DIM, _rope_pair_sum(y), 0.0)
    kn = jnp.dot(ckv, kup_ref[...], preferred_element_type=F32)
    kn_ref[0] = kn.astype(BF16)
    kr_ref[0] = kr.astype(BF16)
    nope_n2 = jnp.dot((kn * kn).astype(BF16), ind_ref[...], preferred_element_type=F32)
    key_n2 = (nope_n2 + jnp.sum(kr * kr, axis=1, keepdims=True)) * NORM_SLACK
    n2_ref[0, 0] = jnp.max(key_n2, axis=0, keepdims=True)
    vt_ref[0] = lax.dot_general(vup_ref[...], ckv, _NT_DIMS,
                                preferred_element_type=F32).astype(BF16)


def _kv_proj(x, tab, w_in, norm_g, k_up, v_up_t):
    b, s, d = x.shape
    t = min(s, ATTN_TILE)
    head_of_lane = np.arange(N_HEADS * QK_NOPE_DIM) // QK_NOPE_DIM
    indicator = jnp.asarray(head_of_lane[:, None] == np.arange(LANES)[None, :], BF16)
    return pl.pallas_call(
        _kv_kernel,
        out_shape=(jax.ShapeDtypeStruct((b, s, N_HEADS * QK_NOPE_DIM), BF16),
                   jax.ShapeDtypeStruct((b, s, LANES), BF16),
                   jax.ShapeDtypeStruct((b, N_HEADS * V_HEAD_DIM, s), BF16),
                   jax.ShapeDtypeStruct((b, s // t, 1, LANES), F32)),
        grid=(b, s // t),
        in_specs=[
            pl.BlockSpec((1, t, d), lambda bi, i: (bi, i, 0)),
            pl.BlockSpec((1, t, LANES), lambda bi, i: (bi, i, 0)),
            _const_spec((d, KV_LORA_RANK + LANES)),
            _const_spec((1, KV_LORA_RANK)),
            _const_spec((KV_LORA_RANK, N_HEADS * QK_NOPE_DIM)),
            _const_spec((N_HEADS * V_HEAD_DIM, KV_LORA_RANK)),
            _const_spec((N_HEADS * QK_NOPE_DIM, LANES)),
        ],
        out_specs=(pl.BlockSpec((1, t, N_HEADS * QK_NOPE_DIM), lambda bi, i: (bi, i, 0)),
                   pl.BlockSpec((1, t, LANES), lambda bi, i: (bi, i, 0)),
                   pl.BlockSpec((1, N_HEADS * V_HEAD_DIM, t), lambda bi, i: (bi, 0, i)),
                   pl.BlockSpec((1, 1, 1, LANES), lambda bi, i: (bi, i, 0, 0))),
        compiler_params=_params(("arbitrary", "arbitrary")),
        name="kv_proj",
    )(x, tab, w_in, norm_g, k_up, v_up_t, indicator)


def _q_kernel(x_ref, tabt_ref, shift_ref, scale_ref, wd_ref, ng_ref, wq_ref, qt_ref, n2_ref):
    h = (x_ref[0] * (1.0 + scale_ref[0]) + shift_ref[0]).astype(BF16)
    c = jnp.dot(h, wd_ref[...], preferred_element_type=F32)
    ms = jnp.mean(c * c, axis=-1, keepdims=True)
    cq = (c * lax.rsqrt(ms + RMS_EPS) * ng_ref[...]).astype(BF16)
    qa = lax.dot_general(wq_ref[...], cq, _NT_DIMS, preferred_element_type=F32)
    tabt = tabt_ref[0]
    cos2, sin2 = tabt[:QK_ROPE_DIM], tabt[QK_ROPE_DIM:]
    t = tabt.shape[1]
    rope0 = N_HEADS * QK_NOPE_DIM
    n2 = []
    for hd in range(N_HEADS):
        nope = qa[hd * LANES:(hd + 1) * LANES] * Q_SCALE
        qt_ref[0, hd * LANES:(hd + 1) * LANES, :] = nope.astype(BF16)
        rope_rows = slice(rope0 + hd * QK_ROPE_DIM, rope0 + (hd + 1) * QK_ROPE_DIM)
        a = qa[rope_rows]
        swapped = jnp.concatenate([a[ROPE_HALF:], a[:ROPE_HALF]], axis=0)
        rot = (a * cos2 + swapped * sin2) * Q_SCALE
        qt_ref[0, rope_rows, :] = rot.astype(BF16)
        query_n2 = (jnp.sum(nope * nope, axis=0, keepdims=True)
                    + jnp.sum(rot * rot, axis=0, keepdims=True))
        folded = query_n2[:, :LANES]
        for c in range(1, t // LANES):
            folded = jnp.maximum(folded, query_n2[:, c * LANES:(c + 1) * LANES])
        n2.append(folded)
    n2_ref[0, 0] = jnp.concatenate(n2, axis=0)


def _q_proj(x, tabt, shift, scale, w_down, norm_g, w_q_t):
    b, s, d = x.shape
    t = min(s, ATTN_TILE)
    mod_spec = pl.BlockSpec((1, 1, d), lambda bi, i: (bi, 0, 0))
    return pl.pallas_call(
        _q_kernel,
        out_shape=(jax.ShapeDtypeStruct((b, Q_ROWS, s), BF16),
                   jax.ShapeDtypeStruct((b, s // t, N_HEADS, LANES), F32)),
        grid=(b, s // t),
        in_specs=[
            pl.BlockSpec((1, t, d), lambda bi, i: (bi, i, 0)),
            pl.BlockSpec((1, LANES, t), lambda bi, i: (bi, 0, i)),
            mod_spec, mod_spec,
            _const_spec((d, Q_LORA_RANK)), _const_spec((1, Q_LORA_RANK)),
            _const_spec((Q_ROWS, Q_LORA_RANK)),
        ],
        out_specs=(pl.BlockSpec((1, Q_ROWS, t), lambda bi, i: (bi, 0, i)),
                   pl.BlockSpec((1, 1, N_HEADS, LANES), lambda bi, i: (bi, i, 0, 0))),
        compiler_params=_params(("arbitrary", "arbitrary")),
        name="q_proj",
    )(x, tabt, shift, scale, w_down, norm_g, w_q_t)


def _attn_kernel(qi_ref, kj_ref, bounded_ref, qt_ref, kn_ref, kr_ref, vt_ref, x_ref, gate_ref,
                 wo_ref, o_ref, m_ref, l_ref, acc_ref):
    p = pl.program_id(1)
    i = qi_ref[p]
    j = kj_ref[p]
    bounded = bounded_ref[pl.program_id(0) * pl.num_programs(1) + p] != 0
    tk = kn_ref.shape[1]
    n_strips = qt_ref.shape[2] // tk
    diag = j - n_strips * i

    @pl.when(j == 0)
    def _():
        m_ref[...] = jnp.full(m_ref.shape, -jnp.inf, F32)
        l_ref[...] = jnp.zeros(l_ref.shape, F32)
        acc_ref[...] = jnp.zeros(acc_ref.shape, F32)

    half = tk // 2
    ones_rows = jnp.ones((BF16_SUBLANES, tk), BF16)

    def strip_cols(s):
        return slice(s * tk, (s + 1) * tk)

    def operands(h, s):
        k = jnp.concatenate([kn_ref[0, :, h * QK_NOPE_DIM:(h + 1) * QK_NOPE_DIM], kr_ref[0]],
                            axis=1)
        rope0 = N_HEADS * QK_NOPE_DIM + h * QK_ROPE_DIM
        q_rope = qt_ref[0, rope0:rope0 + QK_ROPE_DIM, strip_cols(s)]
        qt = jnp.concatenate(
            [qt_ref[0, h * QK_NOPE_DIM:(h + 1) * QK_NOPE_DIM, strip_cols(s)], q_rope, q_rope],
            axis=0)
        return k, qt

    def scores_full(h, s):
        k, qt = operands(h, s)
        return (jnp.dot(k, qt, preferred_element_type=F32),)

    def softmax_full(h, s, st):
        m_prev = m_ref[h, :, strip_cols(s)]
        m_new = jnp.maximum(m_prev, jnp.max(st, axis=0, keepdims=True))
        m_ref[h, :, strip_cols(s)] = m_new
        return jnp.exp2(m_prev - m_new), jnp.exp2((st - m_new).astype(BF16))

    def softmax_bounded(h, s, *sts):
        return (None,) + tuple(jnp.exp2(st).astype(BF16) for st in sts)

    def values_full(vt_ones, e):
        return jnp.dot(vt_ones, e, preferred_element_type=F32)

    def scores_diag(h, s):
        k, qt = operands(h, s)
        tri = (lax.broadcasted_iota(jnp.int32, (half, half), 0)
               <= lax.broadcasted_iota(jnp.int32, (half, half), 1))
        st_lo = jnp.dot(k[:half], qt, preferred_element_type=F32)
        st_hi = jnp.dot(k[half:], qt[:, half:], preferred_element_type=F32)
        st_lo = jnp.concatenate([jnp.where(tri, st_lo[:, :half], -jnp.inf), st_lo[:, half:]],
                                axis=1)
        return st_lo, jnp.where(tri, st_hi, -jnp.inf)

    def softmax_diag(h, s, st_lo, st_hi):
        m_prev = m_ref[h, :, strip_cols(s)]
        mx_lo = jnp.max(st_lo, axis=0, keepdims=True)
        mx_hi = jnp.max(st_hi, axis=0, keepdims=True)
        m_blk = jnp.concatenate([mx_lo[:, :half], jnp.maximum(mx_lo[:, half:], mx_hi)], axis=1)
        m_new = jnp.maximum(m_prev, m_blk)
        m_ref[h, :, strip_cols(s)] = m_new
        return (jnp.exp2(m_prev - m_new), jnp.exp2((st_lo - m_new).astype(BF16)),
                jnp.exp2((st_hi - m_new[:, half:]).astype(BF16)))

    def values_diag(vt_ones, e_lo, e_hi):
        pv_lo = jnp.dot(vt_ones[:, :half], e_lo, preferred_element_type=F32)
        pv_hi = jnp.dot(vt_ones[:, half:], e_hi, preferred_element_type=F32)
        return jnp.concatenate([pv_lo[:, :half], pv_lo[:, half:] + pv_hi], axis=1)

    def step(on_diag_strip, use_max):
        items = [(h, s, s == on_diag_strip) for h in range(N_HEADS)
                 for s in range(max(on_diag_strip, 0), n_strips)]

        def scores(u):
            h, s, on_diag = items[u]
            return (scores_diag if on_diag else scores_full)(h, s)

        def softmax(u, *st):
            h, s, on_diag = items[u]
            if not use_max:
                return softmax_bounded(h, s, *st)
            return (softmax_diag if on_diag else softmax_full)(h, s, *st)

        def accumulate(u, alpha, *e):
            h, s, on_diag = items[u]
            rows = slice(h * V_HEAD_DIM, (h + 1) * V_HEAD_DIM)
            cols = strip_cols(s)
            vt_ones = jnp.concatenate([vt_ref[0, rows, :], ones_rows], axis=0)
            pv = (values_diag if on_diag else values_full)(vt_ones, *e)
            if alpha is None:
                acc_ref[rows, cols] += pv[:V_HEAD_DIM]
                l_ref[h, :, cols] += pv[V_HEAD_DIM:V_HEAD_DIM + 1]
            else:
                acc_ref[rows, cols] = alpha * acc_ref[rows, cols] + pv[:V_HEAD_DIM]
                l_ref[h, :, cols] = alpha * l_ref[h, :, cols] + pv[V_HEAD_DIM:V_HEAD_DIM + 1]

        n = len(items)
        st = {0: scores(0), 1: scores(1)}
        pe = {0: softmax(0, *st.pop(0))}
        for u in range(n):
            accumulate(u, *pe.pop(u))
            if u + 2 < n:
                st[u + 2] = scores(u + 2)
            if u + 1 < n:
                pe[u + 1] = softmax(u + 1, *st.pop(u + 1))

    for use_max in (False, True):
        path = bounded if not use_max else jnp.logical_not(bounded)

        @pl.when(jnp.logical_and(diag < 0, path))
        def _():
            step(-1, use_max)

        for on_diag_strip in range(n_strips):
            @pl.when(jnp.logical_and(diag == on_diag_strip, path))
            def _():
                step(on_diag_strip, use_max)

    @pl.when(diag == n_strips - 1)
    def _():
        o = jnp.concatenate(
            [(acc_ref[h * V_HEAD_DIM:(h + 1) * V_HEAD_DIM, :] / l_ref[h]).T.astype(BF16)
             for h in range(N_HEADS)], axis=-1)
        y = jnp.dot(o, wo_ref[...], preferred_element_type=F32)
        o_ref[0] = DEEPNORM_ALPHA * x_ref[0] + gate_ref[0] * y


def _score_bounded(q_norm2, k_norm2):
    q_max = jnp.sqrt(jnp.max(q_norm2, axis=-1))
    k_max = lax.cummax(jnp.sqrt(k_norm2[:, :, 0, :N_HEADS]), axis=1)
    return (jnp.max(q_max * k_max, axis=-1) <= SCORE_BOUND).astype(jnp.int32)


def _attn_layer(x, qt, k_nope, k_rope, vt, bounded, gate, w_out):
    b, s, d = x.shape
    tq = min(s, ATTN_TILE)
    tk = min(tq, ATTN_KEY_TILE)
    n_strips = tq // tk
    pairs = [(i, j) for i in range(s // tq) for j in range(n_strips * (i + 1))]
    qi_np = np.array([p[0] for p in pairs], np.int32)
    qi = jnp.asarray(qi_np)
    kj = jnp.asarray(np.array([p[1] for p in pairs], np.int32))
    bounded_steps = bounded[:, qi_np].reshape(-1)
    grid_spec = pltpu.PrefetchScalarGridSpec(
        num_scalar_prefetch=3,
        grid=(b, len(pairs)),
        in_specs=[
            pl.BlockSpec((1, Q_ROWS, tq), lambda bi, p, qi, kj, bd: (bi, 0, qi[p])),
            pl.BlockSpec((1, tk, N_HEADS * QK_NOPE_DIM), lambda bi, p, qi, kj, bd: (bi, kj[p], 0)),
            pl.BlockSpec((1, tk, LANES), lambda bi, p, qi, kj, bd: (bi, kj[p], 0)),
            pl.BlockSpec((1, N_HEADS * V_HEAD_DIM, tk), lambda bi, p, qi, kj, bd: (bi, 0, kj[p])),
            pl.BlockSpec((1, tq, d), lambda bi, p, qi, kj, bd: (bi, qi[p], 0)),
            pl.BlockSpec((1, 1, d), lambda bi, p, qi, kj, bd: (bi, 0, 0)),
            _const_spec((d, d)),
        ],
        out_specs=pl.BlockSpec((1, tq, d), lambda bi, p, qi, kj, bd: (bi, qi[p], 0)),
        scratch_shapes=[
            pltpu.VMEM((N_HEADS, 1, tq), F32),
            pltpu.VMEM((N_HEADS, 1, tq), F32),
            pltpu.VMEM((N_HEADS * V_HEAD_DIM, tq), F32),
        ],
    )
    return pl.pallas_call(
        _attn_kernel,
        out_shape=jax.ShapeDtypeStruct((b, s, d), F32),
        grid_spec=grid_spec,
        compiler_params=_params(("arbitrary", "arbitrary")),
        name="attn_layer",
    )(qi, kj, bounded_steps, qt, k_nope, k_rope, vt, x, gate, w_out)


def _swap_halves(w):
    half = w.shape[-1] // 2
    return jnp.concatenate([w[..., half:], w[..., :half]], axis=-1)


def _regroup_q_up(w):
    r = w.shape[0]
    w = w.reshape(r, N_HEADS, QK_NOPE_DIM + QK_ROPE_DIM)
    nope = w[..., :QK_NOPE_DIM].reshape(r, N_HEADS * QK_NOPE_DIM)
    rope = w[..., QK_NOPE_DIM:].reshape(r, N_HEADS * QK_ROPE_DIM)
    return jnp.concatenate([nope, rope], axis=-1).T.astype(BF16)


def kernel(x, c, positions, ada_w, ada_b, ln_g, ln_b, mlp_w1, mlp_w2, pool_w, pool_scale,
           q_down_w, q_norm_g, q_up_w, attn_out_w, kv_in_w, kv_norm_g, k_up_w, v_up_w):
    b, s, d = x.shape
    mod = _adaln_mod(c, ada_w, ada_b).reshape(DEPTH, b, N_MOD, 1, d)
    tab, tabt = _rope_table(positions)

    kv_w_in = jnp.concatenate([kv_in_w, _swap_halves(kv_in_w[:, KV_LORA_RANK:])],
                              axis=-1).astype(BF16)

    k_nope = k_rope = vt = k_n2 = None
    for l in range(DEPTH):
        shift1, scale1, gate1, shift2, scale2, gate2 = (mod[l, :, m] for m in range(N_MOD))
        g1, b1 = ln_g[l, 0].reshape(1, d), ln_b[l, 0].reshape(1, d)
        g2, b2 = ln_g[l, 1].reshape(1, d), ln_b[l, 1].reshape(1, d)
        if l < N_A_LAYERS:
            xz = x
            pool_args = (shift1, scale1, gate1, pool_w[l].astype(BF16),
                         pool_scale[l].reshape(1, d))
        else:
            if l == N_A_LAYERS:
                k_nope, k_rope, vt, k_n2 = _kv_proj(
                    x, tab, kv_w_in, kv_norm_g.reshape(1, KV_LORA_RANK),
                    k_up_w.astype(BF16), v_up_w.T.astype(BF16))
            jl = l - N_A_LAYERS
            qt, q_n2 = _q_proj(x, tabt, shift1, scale1, q_down_w[jl].astype(BF16),
                               q_norm_g[jl].reshape(1, Q_LORA_RANK), _regroup_q_up(q_up_w[jl]))
            xz = _attn_layer(x, qt, k_nope, k_rope, vt, _score_bounded(q_n2, k_n2), gate1,
                             attn_out_w[jl].astype(BF16))
            pool_args = None
        x = _block_layer(xz, pool_args, g1, b1, shift2, scale2, gate2, mlp_w1[l].astype(BF16),
                         mlp_w2[l].astype(BF16), g2, b2)
    return x
```

```python
import functools

import numpy as np
import jax
import jax.numpy as jnp
from jax import lax
from jax.experimental import pallas as pl
from jax.experimental.pallas import tpu as pltpu

D_MODEL = 1024
DEPTH = 4
N_A_LAYERS = DEPTH // 2
POOL_WINDOWS = (2, 4, 8, 16)
N_POOL_GROUPS = len(POOL_WINDOWS)
POOL_GROUP_DIM = D_MODEL // N_POOL_GROUPS
POOL_HALO = 16
QK_NOPE_DIM = 128
QK_ROPE_DIM = 64
V_HEAD_DIM = 128
N_HEADS = D_MODEL // 128
Q_LORA_RANK = D_MODEL // 2
KV_LORA_RANK = D_MODEL // 4
ROPE_THETA = 10000.0
D_FF = 4 * D_MODEL
N_MOD = 6
DEEPNORM_ALPHA = (2.0 * DEPTH) ** 0.25
LN_EPS = 1e-5
RMS_EPS = 1e-6
ATTN_SCALE = (QK_NOPE_DIM + QK_ROPE_DIM) ** -0.5
Q_SCALE = ATTN_SCALE * float(np.log2(np.e))

LANES = 128
BF16_SUBLANES = 16
ATTN_TILE = 1024
ATTN_KEY_TILE = 512
ROW_BLOCK = 1024
SUB_ROWS = 256
FF_CHUNK = 1024
ADALN_COLS = 1536
SCORE_BOUND = 60.0
NORM_SLACK = 1.0 + 2.0 ** -6
HEAD_W = 2 * LANES
Q_ROWS = N_HEADS * (QK_NOPE_DIM + QK_ROPE_DIM)
ROPE_HALF = QK_ROPE_DIM // 2

VMEM_LIMIT = 56 * 1024 * 1024

F32 = jnp.float32
BF16 = jnp.bfloat16

_NT_DIMS = (((1,), (1,)), ((), ()))


def _params(semantics):
    return pltpu.CompilerParams(dimension_semantics=semantics, vmem_limit_bytes=VMEM_LIMIT)


def _const_spec(shape):
    nd = len(shape)
    return pl.BlockSpec(shape, lambda *_: (0,) * nd, pipeline_mode=pl.Buffered(1))


def _layer_norm(z, g, b):
    mu = jnp.mean(z, axis=-1, keepdims=True)
    zc = z - mu
    var = jnp.mean(zc * zc, axis=-1, keepdims=True)
    return zc * lax.rsqrt(var + LN_EPS) * g + b


def _zero_after(v):
    bits = lax.bitcast_convert_type(v, jnp.uint32)
    bits = lax.shift_right_logical(lax.shift_right_logical(bits, jnp.uint32(16)), jnp.uint32(16))
    return lax.bitcast_convert_type(bits, F32)


def _rope_pair_sum(y):
    return y + pltpu.roll(y, QK_ROPE_DIM, axis=1)


def _adaln_kernel(c_ref, w_ref, b_ref, o_ref):
    c = c_ref[...]
    sc = c * jax.nn.sigmoid(c)
    o_ref[0] = jnp.dot(sc.astype(BF16), w_ref[0].astype(BF16),
                       preferred_element_type=F32) + b_ref[0]


def _adaln_mod(c, ada_w, ada_b):
    b, d = c.shape
    n = ada_w.shape[-1]
    tn = ADALN_COLS
    return pl.pallas_call(
        _adaln_kernel,
        out_shape=jax.ShapeDtypeStruct((DEPTH, b, n), F32),
        grid=(DEPTH, n // tn),
        in_specs=[
            pl.BlockSpec((b, d), lambda l, j: (0, 0)),
            pl.BlockSpec((1, d, tn), lambda l, j: (l, 0, j)),
            pl.BlockSpec((1, 1, tn), lambda l, j: (l, 0, j)),
        ],
        out_specs=pl.BlockSpec((1, b, tn), lambda l, j: (l, 0, j)),
        compiler_params=_params(("arbitrary", "arbitrary")),
        name="adaln_mod",
    )(c, ada_w, ada_b.reshape(DEPTH, 1, n))


def _rope_table_kernel(pos_ref, freq_ref, o_ref, ot_ref):
    pos = pos_ref[0].astype(F32)
    ang = freq_ref[...] * pos
    cos = jnp.cos(ang)
    sin = jnp.sin(ang)
    tab = jnp.concatenate([cos, cos, -sin, sin], axis=0)
    ot_ref[0] = tab
    o_ref[0] = tab.T


def _rope_table(positions):
    b, s = positions.shape
    t = min(s, ROW_BLOCK)
    inv_freq = ROPE_THETA ** (-jnp.arange(0, QK_ROPE_DIM, 2, dtype=F32) / QK_ROPE_DIM)
    return pl.pallas_call(
        _rope_table_kernel,
        out_shape=(jax.ShapeDtypeStruct((b, s, LANES), F32),
                   jax.ShapeDtypeStruct((b, LANES, s), F32)),
        grid=(b, s // t),
        in_specs=[
            pl.BlockSpec((1, 1, t), lambda bi, i: (bi, 0, i)),
            pl.BlockSpec((ROPE_HALF, 1), lambda bi, i: (0, 0)),
        ],
        out_specs=(pl.BlockSpec((1, t, LANES), lambda bi, i: (bi, i, 0)),
                   pl.BlockSpec((1, LANES, t), lambda bi, i: (bi, 0, i))),
        compiler_params=_params(("arbitrary", "arbitrary")),
        name="rope_table",
    )(positions.reshape(b, 1, s), inv_freq.reshape(ROPE_HALF, 1))


def _after(v, after):
    return v if after is None else v + _zero_after(after[0:1, :])


def _block_kernel(*refs, pooled, ff_chunk, sub_rows):
    if pooled:
        (x_ref, xh_ref, shift1_ref, scale1_ref, gate1_ref, wp_ref, ps_ref, g1_ref, b1_ref,
         shift_ref, scale_ref, gate_ref, w1_ref, w2_ref, g2_ref, b2_ref,
         o_ref, x1_ref, h_ref, hm_ref) = refs
        in_ref = x_ref
    else:
        (in_ref, g1_ref, b1_ref, shift_ref, scale_ref, gate_ref, w1_ref, w2_ref, g2_ref, b2_ref,
         o_ref, x1_ref, h_ref) = refs
    t = in_ref.shape[1]
    n_sub = t // sub_rows
    n_chunks = D_FF // ff_chunk
    part_rows = sub_rows // n_chunks
    mul = 1.0 + scale_ref[0]
    shift = shift_ref[0]

    def sub(r):
        return slice(r * sub_rows, (r + 1) * sub_rows)

    def normalise(r, lo, hi, after):
        rows = slice(r * sub_rows + lo, r * sub_rows + hi)
        z = x1_ref[rows, :] if pooled else in_ref[0, rows, :]
        x1_ref[rows, :] = _layer_norm(z, _after(g1_ref[...], after), b1_ref[...])

    def modulate(r, lo, hi, after):
        rows = slice(r * sub_rows + lo, r * sub_rows + hi)
        h_ref[rows, :] = (x1_ref[rows, :] * _after(mul, after) + shift).astype(BF16)

    if pooled:
        i = pl.program_id(1)
        mul1 = 1.0 + scale1_ref[0]
        shift1 = shift1_ref[0]
        pending = {}

        def window_mean_minus_token(r, g, after=None):
            w = POOL_WINDOWS[g]
            cols = slice(g * POOL_GROUP_DIM, (g + 1) * POOL_GROUP_DIM)
            level = hm_ref[:, cols]
            if after is not None:
                level = level + _zero_after(after[0:1, cols])
            span = 1
            while span < w:
                level = level + pltpu.roll(level, span, axis=0)
                span *= 2
            row = i * t + r * sub_rows + lax.broadcasted_iota(jnp.int32, (sub_rows, 1), 0)
            cnt = jnp.minimum(row + 1, w).astype(F32)
            return (level[POOL_HALO:] / cnt - hm_ref[POOL_HALO:, cols]).astype(BF16)

        def pool_open(r, after):
            if r == 0:
                hist = jnp.where(i == 0, 0.0, xh_ref[0] * mul1 + shift1)
            else:
                hist = hm_ref[sub_rows:, :]
            hm_ref[:POOL_HALO, :] = hist
            hm_ref[POOL_HALO:, :] = x_ref[0, sub(r), :] * _after(mul1, after) + shift1
            pending[r] = [window_mean_minus_token(r, g) for g in range(2)]

        def pool_wide(r, after):
            pending[r] += [window_mean_minus_token(r, g, after) for g in range(2, N_POOL_GROUPS)]

        def pool_close(r, after):
            ys = pending.pop(r)
            mixed = jnp.concatenate(
                [jnp.dot(y, wp_ref[g], preferred_element_type=F32) for g, y in enumerate(ys)],
                axis=-1) * _after(ps_ref[...], after)
            x1_ref[sub(r), :] = DEEPNORM_ALPHA * x_ref[0, sub(r), :] + gate1_ref[0] * mixed

        def norm_mod(r, after):
            normalise(r, 0, sub_rows, after)
            modulate(r, 0, sub_rows, None)

        stages = [pool_open, pool_wide, pool_close, norm_mod]
        assert len(stages) <= n_chunks

        def enter(r, c, after):
            if c < len(stages):
                stages[c](r, after)
    else:
        def enter(r, c, after):
            normalise(r, c * part_rows, (c + 1) * part_rows, after)
            modulate(r, c * part_rows, (c + 1) * part_rows, after)

    def ff_part(h, c):
        cols = slice(c * ff_chunk, (c + 1) * ff_chunk)
        a = jnp.maximum(jnp.dot(h, w1_ref[:, cols], preferred_element_type=F32), 0.0)
        return jnp.dot((a * a).astype(BF16), w2_ref[cols, :], preferred_element_type=F32)

    def finish(r, y, lo, hi, after=None):
        rows = slice(r * sub_rows + lo, r * sub_rows + hi)
        z2 = DEEPNORM_ALPHA * x1_ref[rows, :] + _after(gate_ref[0], after) * y[lo:hi]
        o_ref[0, rows, :] = _layer_norm(z2, g2_ref[...], b2_ref[...])

    for c in range(n_chunks):
        enter(0, c, None)
    prev = None
    for r in range(n_sub):
        h = h_ref[sub(r), :]
        y = None
        for c in range(n_chunks):
            part = ff_part(h, c)
            y = part if y is None else y + part
            if r + 1 < n_sub:
                enter(r + 1, c, part)
            if prev is not None:
                finish(r - 1, prev, c * part_rows, (c + 1) * part_rows, after=part)
        prev = y
    finish(n_sub - 1, prev, 0, sub_rows)


def _block_layer(xz, pool_args, ln1_g, ln1_b, shift, scale, gate, w1, w2, ln2_g, ln2_b):
    b, s, d = xz.shape
    t = min(s, ROW_BLOCK)
    sub_rows = min(t, SUB_ROWS)
    mod_spec = pl.BlockSpec((1, 1, d), lambda bi, i: (bi, 0, 0))
    row_spec = pl.BlockSpec((1, t, d), lambda bi, i: (bi, i, 0))
    tail_specs = [
        _const_spec((1, d)), _const_spec((1, d)),
        mod_spec, mod_spec, mod_spec,
        _const_spec((d, D_FF)), _const_spec((D_FF, d)),
        _const_spec((1, d)), _const_spec((1, d)),
    ]
    tail_args = (ln1_g, ln1_b, shift, scale, gate, w1, w2, ln2_g, ln2_b)
    scratch = [pltpu.VMEM((t, d), F32), pltpu.VMEM((t, d), BF16)]
    if pool_args is None:
        in_specs, args = [row_spec] + tail_specs, (xz,) + tail_args
    else:
        halo_blocks = t // POOL_HALO
        in_specs = [
            row_spec,
            pl.BlockSpec((1, POOL_HALO, d),
                         lambda bi, i: (bi, jnp.maximum(i * halo_blocks - 1, 0), 0)),
            mod_spec, mod_spec, mod_spec,
            _const_spec((N_POOL_GROUPS, POOL_GROUP_DIM, POOL_GROUP_DIM)),
            _const_spec((1, d)),
        ] + tail_specs
        args = (xz, xz) + tuple(pool_args) + tail_args
        scratch = scratch + [pltpu.VMEM((POOL_HALO + sub_rows, d), F32)]
    return pl.pallas_call(
        functools.partial(_block_kernel, pooled=pool_args is not None, ff_chunk=FF_CHUNK,
                          sub_rows=sub_rows),
        out_shape=jax.ShapeDtypeStruct((b, s, d), F32),
        grid=(b, s // t),
        in_specs=in_specs,
        out_specs=row_spec,
        scratch_shapes=scratch,
        compiler_params=_params(("arbitrary", "arbitrary")),
        name="pool_mlp_layer" if pool_args is not None else "mlp_layer",
    )(*args)


def _kv_kernel(x_ref, tab_ref, win_ref, ng_ref, kup_ref, vup_ref, ind_ref,
               kn_ref, kr_ref, vt_ref, n2_ref):
    xb = x_ref[0].astype(BF16)
    cy = jnp.dot(xb, win_ref[...], preferred_element_type=F32)
    c = cy[:, :KV_LORA_RANK]
    ms = jnp.mean(c * c, axis=-1, keepdims=True)
    ckv = (c * lax.rsqrt(ms + RMS_EPS) * ng_ref[...]).astype(BF16)
    y = cy[:, KV_LORA_RANK:] * tab_ref[0]
    lane = lax.broadcasted_iota(jnp.int32, y.shape, 1)
    kr = jnp.where(lane < QK_ROPE_DIM, _rope_pair_sum(y), 0.0)
    kn = jnp.dot(ckv, kup_ref[...], preferred_element_type=F32)
    kn_ref[0] = kn.astype(BF16)
    kr_ref[0] = kr.astype(BF16)
    nope_n2 = jnp.dot((kn * kn).astype(BF16), ind_ref[...], preferred_element_type=F32)
    key_n2 = (nope_n2 + jnp.sum(kr * kr, axis=1, keepdims=True)) * NORM_SLACK
    n2_ref[0, 0] = jnp.max(key_n2, axis=0, keepdims=True)
    vt_ref[0] = lax.dot_general(vup_ref[...], ckv, _NT_DIMS,
                                preferred_element_type=F32).astype(BF16)


def _kv_proj(x, tab, w_in, norm_g, k_up, v_up_t):
    b, s, d = x.shape
    t = min(s, ATTN_TILE)
    head_of_lane = np.arange(N_HEADS * QK_NOPE_DIM) // QK_NOPE_DIM
    indicator = jnp.asarray(head_of_lane[:, None] == np.arange(LANES)[None, :], BF16)
    return pl.pallas_call(
        _kv_kernel,
        out_shape=(jax.ShapeDtypeStruct((b, s, N_HEADS * QK_NOPE_DIM), BF16),
                   jax.ShapeDtypeStruct((b, s, LANES), BF16),
                   jax.ShapeDtypeStruct((b, N_HEADS * V_HEAD_DIM, s), BF16),
                   jax.ShapeDtypeStruct((b, s // t, 1, LANES), F32)),
        grid=(b, s // t),
        in_specs=[
            pl.BlockSpec((1, t, d), lambda bi, i: (bi, i, 0)),
            pl.BlockSpec((1, t, LANES), lambda bi, i: (bi, i, 0)),
            _const_spec((d, KV_LORA_RANK + LANES)),
            _const_spec((1, KV_LORA_RANK)),
            _const_spec((KV_LORA_RANK, N_HEADS * QK_NOPE_DIM)),
            _const_spec((N_HEADS * V_HEAD_DIM, KV_LORA_RANK)),
            _const_spec((N_HEADS * QK_NOPE_DIM, LANES)),
        ],
        out_specs=(pl.BlockSpec((1, t, N_HEADS * QK_NOPE_DIM), lambda bi, i: (bi, i, 0)),
                   pl.BlockSpec((1, t, LANES), lambda bi, i: (bi, i, 0)),
                   pl.BlockSpec((1, N_HEADS * V_HEAD_DIM, t), lambda bi, i: (bi, 0, i)),
                   pl.BlockSpec((1, 1, 1, LANES), lambda bi, i: (bi, i, 0, 0))),
        compiler_params=_params(("arbitrary", "arbitrary")),
        name="kv_proj",
    )(x, tab, w_in, norm_g, k_up, v_up_t, indicator)


def _q_kernel(x_ref, tabt_ref, shift_ref, scale_ref, wd_ref, ng_ref, wq_ref, qt_ref, n2_ref):
    h = (x_ref[0] * (1.0 + scale_ref[0]) + shift_ref[0]).astype(BF16)
    c = jnp.dot(h, wd_ref[...], preferred_element_type=F32)
    ms = jnp.mean(c * c, axis=-1, keepdims=True)
    cq = (c * lax.rsqrt(ms + RMS_EPS) * ng_ref[...]).astype(BF16)
    qa = lax.dot_general(wq_ref[...], cq, _NT_DIMS, preferred_element_type=F32)
    tabt = tabt_ref[0]
    cos2, sin2 = tabt[:QK_ROPE_DIM], tabt[QK_ROPE_DIM:]
    t = tabt.shape[1]
    rope0 = N_HEADS * QK_NOPE_DIM
    n2 = []
    for hd in range(N_HEADS):
        nope = qa[hd * LANES:(hd + 1) * LANES] * Q_SCALE
        qt_ref[0, hd * LANES:(hd + 1) * LANES, :] = nope.astype(BF16)
        rope_rows = slice(rope0 + hd * QK_ROPE_DIM, rope0 + (hd + 1) * QK_ROPE_DIM)
        a = qa[rope_rows]
        swapped = jnp.concatenate([a[ROPE_HALF:], a[:ROPE_HALF]], axis=0)
        rot = (a * cos2 + swapped * sin2) * Q_SCALE
        qt_ref[0, rope_rows, :] = rot.astype(BF16)
        query_n2 = (jnp.sum(nope * nope, axis=0, keepdims=True)
                    + jnp.sum(rot * rot, axis=0, keepdims=True))
        folded = query_n2[:, :LANES]
        for c in range(1, t // LANES):
            folded = jnp.maximum(folded, query_n2[:, c * LANES:(c + 1) * LANES])
        n2.append(folded)
    n2_ref[0, 0] = jnp.concatenate(n2, axis=0)


def _q_proj(x, tabt, shift, scale, w_down, norm_g, w_q_t):
    b, s, d = x.shape
    t = min(s, ATTN_TILE)
    mod_spec = pl.BlockSpec((1, 1, d), lambda bi, i: (bi, 0, 0))
    return pl.pallas_call(
        _q_kernel,
        out_shape=(jax.ShapeDtypeStruct((b, Q_ROWS, s), BF16),
                   jax.ShapeDtypeStruct((b, s // t, N_HEADS, LANES), F32)),
        grid=(b, s // t),
        in_specs=[
            pl.BlockSpec((1, t, d), lambda bi, i: (bi, i, 0)),
            pl.BlockSpec((1, LANES, t), lambda bi, i: (bi, 0, i)),
            mod_spec, mod_spec,
            _const_spec((d, Q_LORA_RANK)), _const_spec((1, Q_LORA_RANK)),
            _const_spec((Q_ROWS, Q_LORA_RANK)),
        ],
        out_specs=(pl.BlockSpec((1, Q_ROWS, t), lambda bi, i: (bi, 0, i)),
                   pl.BlockSpec((1, 1, N_HEADS, LANES), lambda bi, i: (bi, i, 0, 0))),
        compiler_params=_params(("arbitrary", "arbitrary")),
        name="q_proj",
    )(x, tabt, shift, scale, w_down, norm_g, w_q_t)


def _attn_kernel(qi_ref, kj_ref, bounded_ref, qt_ref, kn_ref, kr_ref, vt_ref, x_ref, gate_ref,
                 wo_ref, o_ref, m_ref, l_ref, acc_ref):
    p = pl.program_id(1)
    i = qi_ref[p]
    j = kj_ref[p]
    bounded = bounded_ref[pl.program_id(0) * pl.num_programs(1) + p] != 0
    tk = kn_ref.shape[1]
    n_strips = qt_ref.shape[2] // tk
    diag = j - n_strips * i

    @pl.when(j == 0)
    def _():
        m_ref[...] = jnp.full(m_ref.shape, -jnp.inf, F32)
        l_ref[...] = jnp.zeros(l_ref.shape, F32)
        acc_ref[...] = jnp.zeros(acc_ref.shape, F32)

    half = tk // 2
    ones_rows = jnp.ones((BF16_SUBLANES, tk), BF16)

    def strip_cols(s):
        return slice(s * tk, (s + 1) * tk)

    def operands(h, s):
        k = jnp.concatenate([kn_ref[0, :, h * QK_NOPE_DIM:(h + 1) * QK_NOPE_DIM], kr_ref[0]],
                            axis=1)
        rope0 = N_HEADS * QK_NOPE_DIM + h * QK_ROPE_DIM
        q_rope = qt_ref[0, rope0:rope0 + QK_ROPE_DIM, strip_cols(s)]
        qt = jnp.concatenate(
            [qt_ref[0, h * QK_NOPE_DIM:(h + 1) * QK_NOPE_DIM, strip_cols(s)], q_rope, q_rope],
            axis=0)
        return k, qt

    def scores_full(h, s):
        k, qt = operands(h, s)
        return (jnp.dot(k, qt, preferred_element_type=F32),)

    def softmax_full(h, s, st):
        m_prev = m_ref[h, :, strip_cols(s)]
        m_new = jnp.maximum(m_prev, jnp.max(st, axis=0, keepdims=True))
        m_ref[h, :, strip_cols(s)] = m_new
        return jnp.exp2(m_prev - m_new), jnp.exp2((st - m_new).astype(BF16))

    def softmax_bounded(h, s, *sts):
        return (None,) + tuple(jnp.exp2(st).astype(BF16) for st in sts)

    def values_full(vt_ones, e):
        return jnp.dot(vt_ones, e, preferred_element_type=F32)

    def scores_diag(h, s):
        k, qt = operands(h, s)
        tri = (lax.broadcasted_iota(jnp.int32, (half, half), 0)
               <= lax.broadcasted_iota(jnp.int32, (half, half), 1))
        st_lo = jnp.dot(k[:half], qt, preferred_element_type=F32)
        st_hi = jnp.dot(k[half:], qt[:, half:], preferred_element_type=F32)
        st_lo = jnp.concatenate([jnp.where(tri, st_lo[:, :half], -jnp.inf), st_lo[:, half:]],
                                axis=1)
        return st_lo, jnp.where(tri, st_hi, -jnp.inf)

    def softmax_diag(h, s, st_lo, st_hi):
        m_prev = m_ref[h, :, strip_cols(s)]
        mx_lo = jnp.max(st_lo, axis=0, keepdims=True)
        mx_hi = jnp.max(st_hi, axis=0, keepdims=True)
        m_blk = jnp.concatenate([mx_lo[:, :half], jnp.maximum(mx_lo[:, half:], mx_hi)], axis=1)
        m_new = jnp.maximum(m_prev, m_blk)
        m_ref[h, :, strip_cols(s)] = m_new
        return (jnp.exp2(m_prev - m_new), jnp.exp2((st_lo - m_new).astype(BF16)),
                jnp.exp2((st_hi - m_new[:, half:]).astype(BF16)))

    def values_diag(vt_ones, e_lo, e_hi):
        pv_lo = jnp.dot(vt_ones[:, :half], e_lo, preferred_element_type=F32)
        pv_hi = jnp.dot(vt_ones[:, half:], e_hi, preferred_element_type=F32)
        return jnp.concatenate([pv_lo[:, :half], pv_lo[:, half:] + pv_hi], axis=1)

    def step(on_diag_strip, use_max):
        items = [(h, s, s == on_diag_strip) for h in range(N_HEADS)
                 for s in range(max(on_diag_strip, 0), n_strips)]

        def scores(u):
            h, s, on_diag = items[u]
            return (scores_diag if on_diag else scores_full)(h, s)

        def softmax(u, *st):
            h, s, on_diag = items[u]
            if not use_max:
                return softmax_bounded(h, s, *st)
            return (softmax_diag if on_diag else softmax_full)(h, s, *st)

        def accumulate(u, alpha, *e):
            h, s, on_diag = items[u]
            rows = slice(h * V_HEAD_DIM, (h + 1) * V_HEAD_DIM)
            cols = strip_cols(s)
            vt_ones = jnp.concatenate([vt_ref[0, rows, :], ones_rows], axis=0)
            pv = (values_diag if on_diag else values_full)(vt_ones, *e)
            if alpha is None:
                acc_ref[rows, cols] += pv[:V_HEAD_DIM]
                l_ref[h, :, cols] += pv[V_HEAD_DIM:V_HEAD_DIM + 1]
            else:
                acc_ref[rows, cols] = alpha * acc_ref[rows, cols] + pv[:V_HEAD_DIM]
                l_ref[h, :, cols] = alpha * l_ref[h, :, cols] + pv[V_HEAD_DIM:V_HEAD_DIM + 1]

        n = len(items)
        st = {0: scores(0), 1: scores(1)}
        pe = {0: softmax(0, *st.pop(0))}
        for u in range(n):
            accumulate(u, *pe.pop(u))
            if u + 2 < n:
                st[u + 2] = scores(u + 2)
            if u + 1 < n:
                pe[u + 1] = softmax(u + 1, *st.pop(u + 1))

    for use_max in (False, True):
        path = bounded if not use_max else jnp.logical_not(bounded)

        @pl.when(jnp.logical_and(diag < 0, path))
        def _():
            step(-1, use_max)

        for on_diag_strip in range(n_strips):
            @pl.when(jnp.logical_and(diag == on_diag_strip, path))
            def _():
                step(on_diag_strip, use_max)

    @pl.when(diag == n_strips - 1)
    def _():
        o = jnp.concatenate(
            [(acc_ref[h * V_HEAD_DIM:(h + 1) * V_HEAD_DIM, :] / l_ref[h]).T.astype(BF16)
             for h in range(N_HEADS)], axis=-1)
        y = jnp.dot(o, wo_ref[...], preferred_element_type=F32)
        o_ref[0] = DEEPNORM_ALPHA * x_ref[0] + gate_ref[0] * y


def _score_bounded(q_norm2, k_norm2):
    q_max = jnp.sqrt(jnp.max(q_norm2, axis=-1))
    k_max = lax.cummax(jnp.sqrt(k_norm2[:, :, 0, :N_HEADS]), axis=1)
    return (jnp.max(q_max * k_max, axis=-1) <= SCORE_BOUND).astype(jnp.int32)


def _attn_layer(x, qt, k_nope, k_rope, vt, bounded, gate, w_out):
    b, s, d = x.shape
    tq = min(s, ATTN_TILE)
    tk = min(tq, ATTN_KEY_TILE)
    n_strips = tq // tk
    pairs = [(i, j) for i in range(s // tq) for j in range(n_strips * (i + 1))]
    qi_np = np.array([p[0] for p in pairs], np.int32)
    qi = jnp.asarray(qi_np)
    kj = jnp.asarray(np.array([p[1] for p in pairs], np.int32))
    bounded_steps = bounded[:, qi_np].reshape(-1)
    grid_spec = pltpu.PrefetchScalarGridSpec(
        num_scalar_prefetch=3,
        grid=(b, len(pairs)),
        in_specs=[
            pl.BlockSpec((1, Q_ROWS, tq), lambda bi, p, qi, kj, bd: (bi, 0, qi[p])),
            pl.BlockSpec((1, tk, N_HEADS * QK_NOPE_DIM), lambda bi, p, qi, kj, bd: (bi, kj[p], 0)),
            pl.BlockSpec((1, tk, LANES), lambda bi, p, qi, kj, bd: (bi, kj[p], 0)),
            pl.BlockSpec((1, N_HEADS * V_HEAD_DIM, tk), lambda bi, p, qi, kj, bd: (bi, 0, kj[p])),
            pl.BlockSpec((1, tq, d), lambda bi, p, qi, kj, bd: (bi, qi[p], 0)),
            pl.BlockSpec((1, 1, d), lambda bi, p, qi, kj, bd: (bi, 0, 0)),
            _const_spec((d, d)),
        ],
        out_specs=pl.BlockSpec((1, tq, d), lambda bi, p, qi, kj, bd: (bi, qi[p], 0)),
        scratch_shapes=[
            pltpu.VMEM((N_HEADS, 1, tq), F32),
            pltpu.VMEM((N_HEADS, 1, tq), F32),
            pltpu.VMEM((N_HEADS * V_HEAD_DIM, tq), F32),
        ],
    )
    return pl.pallas_call(
        _attn_kernel,
        out_shape=jax.ShapeDtypeStruct((b, s, d), F32),
        grid_spec=grid_spec,
        compiler_params=_params(("arbitrary", "arbitrary")),
        name="attn_layer",
    )(qi, kj, bounded_steps, qt, k_nope, k_rope, vt, x, gate, w_out)


def _swap_halves(w):
    half = w.shape[-1] // 2
    return jnp.concatenate([w[..., half:], w[..., :half]], axis=-1)


def _regroup_q_up(w):
    r = w.shape[0]
    w = w.reshape(r, N_HEADS, QK_NOPE_DIM + QK_ROPE_DIM)
    nope = w[..., :QK_NOPE_DIM].reshape(r, N_HEADS * QK_NOPE_DIM)
    rope = w[..., QK_NOPE_DIM:].reshape(r, N_HEADS * QK_ROPE_DIM)
    return jnp.concatenate([nope, rope], axis=-1).T.astype(BF16)


def kernel(x, c, positions, ada_w, ada_b, ln_g, ln_b, mlp_w1, mlp_w2, pool_w, pool_scale,
           q_down_w, q_norm_g, q_up_w, attn_out_w, kv_in_w, kv_norm_g, k_up_w, v_up_w):
    b, s, d = x.shape
    assert d == D_MODEL and s % min(s, ROW_BLOCK) == 0 and s % min(s, ATTN_TILE) == 0
    mod = _adaln_mod(c, ada_w, ada_b).reshape(DEPTH, b, N_MOD, 1, d)
    tab, tabt = _rope_table(positions)

    kv_w_in = jnp.concatenate([kv_in_w, _swap_halves(kv_in_w[:, KV_LORA_RANK:])],
                              axis=-1).astype(BF16)

    k_nope = k_rope = vt = k_n2 = None
    for l in range(DEPTH):
        shift1, scale1, gate1, shift2, scale2, gate2 = (mod[l, :, m] for m in range(N_MOD))
        g1, b1 = ln_g[l, 0].reshape(1, d), ln_b[l, 0].reshape(1, d)
        g2, b2 = ln_g[l, 1].reshape(1, d), ln_b[l, 1].reshape(1, d)
        if l < N_A_LAYERS:
            xz = x
            pool_args = (shift1, scale1, gate1, pool_w[l].astype(BF16),
                         pool_scale[l].reshape(1, d))
        else:
            if l == N_A_LAYERS:
                k_nope, k_rope, vt, k_n2 = _kv_proj(
                    x, tab, kv_w_in, kv_norm_g.reshape(1, KV_LORA_RANK),
                    k_up_w.astype(BF16), v_up_w.T.astype(BF16))
            jl = l - N_A_LAYERS
            qt, q_n2 = _q_proj(x, tabt, shift1, scale1, q_down_w[jl].astype(BF16),
                               q_norm_g[jl].reshape(1, Q_LORA_RANK), _regroup_q_up(q_up_w[jl]))
            xz = _attn_layer(x, qt, k_nope, k_rope, vt, _score_bounded(q_n2, k_n2), gate1,
                             attn_out_w[jl].astype(BF16))
            pool_args = None
        x = _block_layer(xz, pool_args, g1, b1, shift2, scale2, gate2, mlp_w1[l].astype(BF16),
                         mlp_w2[l].astype(BF16), g2, b2)
    return x
```

```python
import functools

import numpy as np
import jax
import jax.numpy as jnp
from jax import lax
from jax.experimental import pallas as pl
from jax.experimental.pallas import tpu as pltpu

D_MODEL = 1024
DEPTH = 4
N_A_LAYERS = DEPTH // 2
POOL_WINDOWS = (2, 4, 8, 16)
N_POOL_GROUPS = len(POOL_WINDOWS)
POOL_GROUP_DIM = D_MODEL // N_POOL_GROUPS
POOL_HALO = 16
QK_NOPE_DIM = 128
QK_ROPE_DIM = 64
V_HEAD_DIM = 128
N_HEADS = D_MODEL // 128
Q_LORA_RANK = D_MODEL // 2
KV_LORA_RANK = D_MODEL // 4
ROPE_THETA = 10000.0
D_FF = 4 * D_MODEL
N_MOD = 6
DEEPNORM_ALPHA = (2.0 * DEPTH) ** 0.25
LN_EPS = 1e-5
RMS_EPS = 1e-6
ATTN_SCALE = (QK_NOPE_DIM + QK_ROPE_DIM) ** -0.5
Q_SCALE = ATTN_SCALE * float(np.log2(np.e))

LANES = 128
BF16_SUBLANES = 16
ATTN_TILE = 1024
ATTN_KEY_TILE = 512
ROW_BLOCK = 1024
SUB_ROWS = 256
FF_CHUNK = 1024
ADALN_COLS = 1536
SCORE_BOUND = 60.0
VALUE_BOUND = 2.0 ** 40
NORM_SLACK = 1.0 + 2.0 ** -6
HEAD_W = 2 * LANES
Q_ROWS = N_HEADS * (QK_NOPE_DIM + QK_ROPE_DIM)
ROPE_HALF = QK_ROPE_DIM // 2

VMEM_LIMIT = 56 * 1024 * 1024

F32 = jnp.float32
BF16 = jnp.bfloat16

_NT_DIMS = (((1,), (1,)), ((), ()))


def _params(semantics):
    return pltpu.CompilerParams(dimension_semantics=semantics, vmem_limit_bytes=VMEM_LIMIT)


def _const_spec(shape):
    nd = len(shape)
    return pl.BlockSpec(shape, lambda *_: (0,) * nd, pipeline_mode=pl.Buffered(1))


def _layer_norm(z, g, b):
    mu = jnp.mean(z, axis=-1, keepdims=True)
    zc = z - mu
    var = jnp.mean(zc * zc, axis=-1, keepdims=True)
    return zc * lax.rsqrt(var + LN_EPS) * g + b


def _zero_after(v):
    bits = lax.bitcast_convert_type(v, jnp.uint32)
    bits = lax.shift_right_logical(lax.shift_right_logical(bits, jnp.uint32(16)), jnp.uint32(16))
    return lax.bitcast_convert_type(bits, F32)


def _rope_pair_sum(y):
    return y + pltpu.roll(y, QK_ROPE_DIM, axis=1)


def _adaln_kernel(c_ref, w_ref, b_ref, o_ref):
    c = c_ref[...]
    sc = c * jax.nn.sigmoid(c)
    o_ref[0] = jnp.dot(sc.astype(BF16), w_ref[0].astype(BF16),
                       preferred_element_type=F32) + b_ref[0]


def _adaln_mod(c, ada_w, ada_b):
    b, d = c.shape
    n = ada_w.shape[-1]
    tn = ADALN_COLS
    return pl.pallas_call(
        _adaln_kernel,
        out_shape=jax.ShapeDtypeStruct((DEPTH, b, n), F32),
        grid=(DEPTH, n // tn),
        in_specs=[
            pl.BlockSpec((b, d), lambda l, j: (0, 0)),
            pl.BlockSpec((1, d, tn), lambda l, j: (l, 0, j)),
            pl.BlockSpec((1, 1, tn), lambda l, j: (l, 0, j)),
        ],
        out_specs=pl.BlockSpec((1, b, tn), lambda l, j: (l, 0, j)),
        compiler_params=_params(("arbitrary", "arbitrary")),
        name="adaln_mod",
    )(c, ada_w, ada_b.reshape(DEPTH, 1, n))


def _rope_table_kernel(pos_ref, freq_ref, o_ref, ot_ref):
    pos = pos_ref[0].astype(F32)
    ang = freq_ref[...] * pos
    cos = jnp.cos(ang)
    sin = jnp.sin(ang)
    tab = jnp.concatenate([cos, cos, -sin, sin], axis=0)
    ot_ref[0] = tab
    o_ref[0] = tab.T


def _rope_table(positions):
    b, s = positions.shape
    t = min(s, ROW_BLOCK)
    inv_freq = ROPE_THETA ** (-jnp.arange(0, QK_ROPE_DIM, 2, dtype=F32) / QK_ROPE_DIM)
    return pl.pallas_call(
        _rope_table_kernel,
        out_shape=(jax.ShapeDtypeStruct((b, s, LANES), F32),
                   jax.ShapeDtypeStruct((b, LANES, s), F32)),
        grid=(b, s // t),
        in_specs=[
            pl.BlockSpec((1, 1, t), lambda bi, i: (bi, 0, i)),
            pl.BlockSpec((ROPE_HALF, 1), lambda bi, i: (0, 0)),
        ],
        out_specs=(pl.BlockSpec((1, t, LANES), lambda bi, i: (bi, i, 0)),
                   pl.BlockSpec((1, LANES, t), lambda bi, i: (bi, 0, i))),
        compiler_params=_params(("arbitrary", "arbitrary")),
        name="rope_table",
    )(positions.reshape(b, 1, s), inv_freq.reshape(ROPE_HALF, 1))


def _after(v, after):
    return v if after is None else v + _zero_after(after[0:1, :])


def _block_kernel(*refs, pooled, ff_chunk, sub_rows):
    if pooled:
        (x_ref, xh_ref, shift1_ref, scale1_ref, gate1_ref, wp_ref, ps_ref, g1_ref, b1_ref,
         shift_ref, scale_ref, gate_ref, w1_ref, w2_ref, g2_ref, b2_ref,
         o_ref, x1_ref, h_ref, hm_ref) = refs
        in_ref = x_ref
    else:
        (in_ref, g1_ref, b1_ref, shift_ref, scale_ref, gate_ref, w1_ref, w2_ref, g2_ref, b2_ref,
         o_ref, x1_ref, h_ref) = refs
    t = in_ref.shape[1]
    n_sub = t // sub_rows
    n_chunks = D_FF // ff_chunk
    part_rows = sub_rows // n_chunks
    mul = 1.0 + scale_ref[0]
    shift = shift_ref[0]

    def sub(r):
        return slice(r * sub_rows, (r + 1) * sub_rows)

    def normalise(r, lo, hi, after):
        rows = slice(r * sub_rows + lo, r * sub_rows + hi)
        z = x1_ref[rows, :] if pooled else in_ref[0, rows, :]
        x1_ref[rows, :] = _layer_norm(z, _after(g1_ref[...], after), b1_ref[...])

    def modulate(r, lo, hi, after):
        rows = slice(r * sub_rows + lo, r * sub_rows + hi)
        h_ref[rows, :] = (x1_ref[rows, :] * _after(mul, after) + shift).astype(BF16)

    if pooled:
        i = pl.program_id(1)
        mul1 = 1.0 + scale1_ref[0]
        shift1 = shift1_ref[0]
        pending = {}

        def window_mean_minus_token(r, g, after=None):
            w = POOL_WINDOWS[g]
            cols = slice(g * POOL_GROUP_DIM, (g + 1) * POOL_GROUP_DIM)
            level = hm_ref[:, cols]
            if after is not None:
                level = level + _zero_after(after[0:1, cols])
            span = 1
            while span < w:
                level = level + pltpu.roll(level, span, axis=0)
                span *= 2
            row = i * t + r * sub_rows + lax.broadcasted_iota(jnp.int32, (sub_rows, 1), 0)
            cnt = jnp.minimum(row + 1, w).astype(F32)
            return (level[POOL_HALO:] / cnt - hm_ref[POOL_HALO:, cols]).astype(BF16)

        def pool_open(r, after):
            if r == 0:
                hist = jnp.where(i == 0, 0.0, xh_ref[0] * mul1 + shift1)
            else:
                hist = hm_ref[sub_rows:, :]
            hm_ref[:POOL_HALO, :] = hist
            hm_ref[POOL_HALO:, :] = x_ref[0, sub(r), :] * _after(mul1, after) + shift1
            pending[r] = [window_mean_minus_token(r, g) for g in range(2)]

        def pool_wide(r, after):
            pending[r] += [window_mean_minus_token(r, g, after) for g in range(2, N_POOL_GROUPS)]

        def pool_close(r, after):
            ys = pending.pop(r)
            mixed = jnp.concatenate(
                [jnp.dot(y, wp_ref[g], preferred_element_type=F32) for g, y in enumerate(ys)],
                axis=-1) * _after(ps_ref[...], after)
            x1_ref[sub(r), :] = DEEPNORM_ALPHA * x_ref[0, sub(r), :] + gate1_ref[0] * mixed

        def norm_mod(r, after):
            normalise(r, 0, sub_rows, after)
            modulate(r, 0, sub_rows, None)

        stages = [pool_open, pool_wide, pool_close, norm_mod]
        assert len(stages) <= n_chunks

        def enter(r, c, after):
            if c < len(stages):
                stages[c](r, after)
    else:
        def enter(r, c, after):
            normalise(r, c * part_rows, (c + 1) * part_rows, after)
            modulate(r, c * part_rows, (c + 1) * part_rows, after)

    def ff_part(h, c):
        cols = slice(c * ff_chunk, (c + 1) * ff_chunk)
        a = jnp.maximum(jnp.dot(h, w1_ref[:, cols], preferred_element_type=F32), 0.0)
        return jnp.dot((a * a).astype(BF16), w2_ref[cols, :], preferred_element_type=F32)

    def finish(r, y, lo, hi, after=None):
        rows = slice(r * sub_rows + lo, r * sub_rows + hi)
        z2 = DEEPNORM_ALPHA * x1_ref[rows, :] + _after(gate_ref[0], after) * y[lo:hi]
        o_ref[0, rows, :] = _layer_norm(z2, g2_ref[...], b2_ref[...])

    for c in range(n_chunks):
        enter(0, c, None)
    prev = None
    for r in range(n_sub):
        h = h_ref[sub(r), :]
        y = None
        for c in range(n_chunks):
            part = ff_part(h, c)
            y = part if y is None else y + part
            if r + 1 < n_sub:
                enter(r + 1, c, part)
            if prev is not None:
                finish(r - 1, prev, c * part_rows, (c + 1) * part_rows, after=part)
        prev = y
    finish(n_sub - 1, prev, 0, sub_rows)


def _block_layer(xz, pool_args, ln1_g, ln1_b, shift, scale, gate, w1, w2, ln2_g, ln2_b):
    b, s, d = xz.shape
    t = min(s, ROW_BLOCK)
    sub_rows = min(t, SUB_ROWS)
    mod_spec = pl.BlockSpec((1, 1, d), lambda bi, i: (bi, 0, 0))
    row_spec = pl.BlockSpec((1, t, d), lambda bi, i: (bi, i, 0))
    tail_specs = [
        _const_spec((1, d)), _const_spec((1, d)),
        mod_spec, mod_spec, mod_spec,
        _const_spec((d, D_FF)), _const_spec((D_FF, d)),
        _const_spec((1, d)), _const_spec((1, d)),
    ]
    tail_args = (ln1_g, ln1_b, shift, scale, gate, w1, w2, ln2_g, ln2_b)
    scratch = [pltpu.VMEM((t, d), F32), pltpu.VMEM((t, d), BF16)]
    if pool_args is None:
        in_specs, args = [row_spec] + tail_specs, (xz,) + tail_args
    else:
        halo_blocks = t // POOL_HALO
        in_specs = [
            row_spec,
            pl.BlockSpec((1, POOL_HALO, d),
                         lambda bi, i: (bi, jnp.maximum(i * halo_blocks - 1, 0), 0)),
            mod_spec, mod_spec, mod_spec,
            _const_spec((N_POOL_GROUPS, POOL_GROUP_DIM, POOL_GROUP_DIM)),
            _const_spec((1, d)),
        ] + tail_specs
        args = (xz, xz) + tuple(pool_args) + tail_args
        scratch = scratch + [pltpu.VMEM((POOL_HALO + sub_rows, d), F32)]
    return pl.pallas_call(
        functools.partial(_block_kernel, pooled=pool_args is not None, ff_chunk=FF_CHUNK,
                          sub_rows=sub_rows),
        out_shape=jax.ShapeDtypeStruct((b, s, d), F32),
        grid=(b, s // t),
        in_specs=in_specs,
        out_specs=row_spec,
        scratch_shapes=scratch,
        compiler_params=_params(("arbitrary", "arbitrary")),
        name="pool_mlp_layer" if pool_args is not None else "mlp_layer",
    )(*args)


def _kv_kernel(x_ref, tab_ref, win_ref, ng_ref, kup_ref, vup_ref, ind_ref,
               kn_ref, kr_ref, vt_ref, n2_ref):
    xb = x_ref[0].astype(BF16)
    cy = jnp.dot(xb, win_ref[...], preferred_element_type=F32)
    c = cy[:, :KV_LORA_RANK]
    ms = jnp.mean(c * c, axis=-1, keepdims=True)
    ckv = (c * lax.rsqrt(ms + RMS_EPS) * ng_ref[...]).astype(BF16)
    y = cy[:, KV_LORA_RANK:] * tab_ref[0]
    lane = lax.broadcasted_iota(jnp.int32, y.shape, 1)
    kr = jnp.where(lane < QK_ROPE_DIM, _rope_pair_sum(y), 0.0)
    kn = jnp.dot(ckv, kup_ref[...], preferred_element_type=F32)
    kn_ref[0] = kn.astype(BF16)
    kr_ref[0] = kr.astype(BF16)
    nope_n2 = jnp.dot((kn * kn).astype(BF16), ind_ref[...], preferred_element_type=F32)
    key_n2 = (nope_n2 + jnp.sum(kr * kr, axis=1, keepdims=True)) * NORM_SLACK
    n2_ref[0, 0] = jnp.max(key_n2, axis=0, keepdims=True)
    vt_ref[0] = lax.dot_general(vup_ref[...], ckv, _NT_DIMS,
                                preferred_element_type=F32).astype(BF16)


def _kv_proj(x, tab, w_in, norm_g, k_up, v_up_t):
    b, s, d = x.shape
    t = min(s, ATTN_TILE)
    head_of_lane = np.arange(N_HEADS * QK_NOPE_DIM) // QK_NOPE_DIM
    indicator = jnp.asarray(head_of_lane[:, None] == np.arange(LANES)[None, :], BF16)
    return pl.pallas_call(
        _kv_kernel,
        out_shape=(jax.ShapeDtypeStruct((b, s, N_HEADS * QK_NOPE_DIM), BF16),
                   jax.ShapeDtypeStruct((b, s, LANES), BF16),
                   jax.ShapeDtypeStruct((b, N_HEADS * V_HEAD_DIM, s), BF16),
                   jax.ShapeDtypeStruct((b, s // t, 1, LANES), F32)),
        grid=(b, s // t),
        in_specs=[
            pl.BlockSpec((1, t, d), lambda bi, i: (bi, i, 0)),
            pl.BlockSpec((1, t, LANES), lambda bi, i: (bi, i, 0)),
            _const_spec((d, KV_LORA_RANK + LANES)),
            _const_spec((1, KV_LORA_RANK)),
            _const_spec((KV_LORA_RANK, N_HEADS * QK_NOPE_DIM)),
            _const_spec((N_HEADS * V_HEAD_DIM, KV_LORA_RANK)),
            _const_spec((N_HEADS * QK_NOPE_DIM, LANES)),
        ],
        out_specs=(pl.BlockSpec((1, t, N_HEADS * QK_NOPE_DIM), lambda bi, i: (bi, i, 0)),
                   pl.BlockSpec((1, t, LANES), lambda bi, i: (bi, i, 0)),
                   pl.BlockSpec((1, N_HEADS * V_HEAD_DIM, t), lambda bi, i: (bi, 0, i)),
                   pl.BlockSpec((1, 1, 1, LANES), lambda bi, i: (bi, i, 0, 0))),
        compiler_params=_params(("arbitrary", "arbitrary")),
        name="kv_proj",
    )(x, tab, w_in, norm_g, k_up, v_up_t, indicator)


def _q_kernel(x_ref, tabt_ref, shift_ref, scale_ref, wd_ref, ng_ref, wq_ref, qt_ref, n2_ref):
    h = (x_ref[0] * (1.0 + scale_ref[0]) + shift_ref[0]).astype(BF16)
    c = jnp.dot(h, wd_ref[...], preferred_element_type=F32)
    ms = jnp.mean(c * c, axis=-1, keepdims=True)
    cq = (c * lax.rsqrt(ms + RMS_EPS) * ng_ref[...]).astype(BF16)
    qa = lax.dot_general(wq_ref[...], cq, _NT_DIMS, preferred_element_type=F32)
    tabt = tabt_ref[0]
    cos2, sin2 = tabt[:QK_ROPE_DIM], tabt[QK_ROPE_DIM:]
    t = tabt.shape[1]
    rope0 = N_HEADS * QK_NOPE_DIM
    n2 = []
    for hd in range(N_HEADS):
        nope = qa[hd * LANES:(hd + 1) * LANES] * Q_SCALE
        qt_ref[0, hd * LANES:(hd + 1) * LANES, :] = nope.astype(BF16)
        rope_rows = slice(rope0 + hd * QK_ROPE_DIM, rope0 + (hd + 1) * QK_ROPE_DIM)
        a = qa[rope_rows]
        swapped = jnp.concatenate([a[ROPE_HALF:], a[:ROPE_HALF]], axis=0)
        rot = (a * cos2 + swapped * sin2) * Q_SCALE
        qt_ref[0, rope_rows, :] = rot.astype(BF16)
        query_n2 = (jnp.sum(nope * nope, axis=0, keepdims=True)
                    + jnp.sum(rot * rot, axis=0, keepdims=True))
        folded = query_n2[:, :LANES]
        for c in range(1, t // LANES):
            folded = jnp.maximum(folded, query_n2[:, c * LANES:(c + 1) * LANES])
        n2.append(folded)
    n2_ref[0, 0] = jnp.concatenate(n2, axis=0)


def _q_proj(x, tabt, shift, scale, w_down, norm_g, w_q_t):
    b, s, d = x.shape
    t = min(s, ATTN_TILE)
    mod_spec = pl.BlockSpec((1, 1, d), lambda bi, i: (bi, 0, 0))
    return pl.pallas_call(
        _q_kernel,
        out_shape=(jax.ShapeDtypeStruct((b, Q_ROWS, s), BF16),
                   jax.ShapeDtypeStruct((b, s // t, N_HEADS, LANES), F32)),
        grid=(b, s // t),
        in_specs=[
            pl.BlockSpec((1, t, d), lambda bi, i: (bi, i, 0)),
            pl.BlockSpec((1, LANES, t), lambda bi, i: (bi, 0, i)),
            mod_spec, mod_spec,
            _const_spec((d, Q_LORA_RANK)), _const_spec((1, Q_LORA_RANK)),
            _const_spec((Q_ROWS, Q_LORA_RANK)),
        ],
        out_specs=(pl.BlockSpec((1, Q_ROWS, t), lambda bi, i: (bi, 0, i)),
                   pl.BlockSpec((1, 1, N_HEADS, LANES), lambda bi, i: (bi, i, 0, 0))),
        compiler_params=_params(("arbitrary", "arbitrary")),
        name="q_proj",
    )(x, tabt, shift, scale, w_down, norm_g, w_q_t)


def _attn_kernel(qi_ref, kj_ref, bounded_ref, qt_ref, kn_ref, kr_ref, vt_ref, x_ref, gate_ref,
                 wo_ref, o_ref, m_ref, l_ref, acc_ref):
    p = pl.program_id(1)
    i = qi_ref[p]
    j = kj_ref[p]
    bounded = bounded_ref[pl.program_id(0) * pl.num_programs(1) + p] != 0
    tk = kn_ref.shape[1]
    n_strips = qt_ref.shape[2] // tk
    diag = j - n_strips * i

    @pl.when(j == 0)
    def _():
        m_ref[...] = jnp.full(m_ref.shape, -jnp.inf, F32)
        l_ref[...] = jnp.zeros(l_ref.shape, F32)
        acc_ref[...] = jnp.zeros(acc_ref.shape, F32)

    half = tk // 2
    ones_rows = jnp.ones((BF16_SUBLANES, tk), BF16)

    def strip_cols(s):
        return slice(s * tk, (s + 1) * tk)

    def operands(h, s):
        k = jnp.concatenate([kn_ref[0, :, h * QK_NOPE_DIM:(h + 1) * QK_NOPE_DIM], kr_ref[0]],
                            axis=1)
        rope0 = N_HEADS * QK_NOPE_DIM + h * QK_ROPE_DIM
        q_rope = qt_ref[0, rope0:rope0 + QK_ROPE_DIM, strip_cols(s)]
        qt = jnp.concatenate(
            [qt_ref[0, h * QK_NOPE_DIM:(h + 1) * QK_NOPE_DIM, strip_cols(s)], q_rope, q_rope],
            axis=0)
        return k, qt

    def scores_full(h, s):
        k, qt = operands(h, s)
        return (jnp.dot(k, qt, preferred_element_type=F32),)

    def softmax_full(h, s, st):
        m_prev = m_ref[h, :, strip_cols(s)]
        m_new = jnp.maximum(m_prev, jnp.max(st, axis=0, keepdims=True))
        m_ref[h, :, strip_cols(s)] = m_new
        return jnp.exp2(m_prev - m_new), jnp.exp2((st - m_new).astype(BF16))

    def softmax_bounded(h, s, *sts):
        return (None,) + tuple(jnp.exp2(st).astype(BF16) for st in sts)

    def values_full(vt_ones, e):
        return jnp.dot(vt_ones, e, preferred_element_type=F32)

    def scores_diag(h, s):
        k, qt = operands(h, s)
        tri = (lax.broadcasted_iota(jnp.int32, (half, half), 0)
               <= lax.broadcasted_iota(jnp.int32, (half, half), 1))
        st_lo = jnp.dot(k[:half], qt, preferred_element_type=F32)
        st_hi = jnp.dot(k[half:], qt[:, half:], preferred_element_type=F32)
        st_lo = jnp.concatenate([jnp.where(tri, st_lo[:, :half], -jnp.inf), st_lo[:, half:]],
                                axis=1)
        return st_lo, jnp.where(tri, st_hi, -jnp.inf)

    def softmax_diag(h, s, st_lo, st_hi):
        m_prev = m_ref[h, :, strip_cols(s)]
        mx_lo = jnp.max(st_lo, axis=0, keepdims=True)
        mx_hi = jnp.max(st_hi, axis=0, keepdims=True)
        m_blk = jnp.concatenate([mx_lo[:, :half], jnp.maximum(mx_lo[:, half:], mx_hi)], axis=1)
        m_new = jnp.maximum(m_prev, m_blk)
        m_ref[h, :, strip_cols(s)] = m_new
        return (jnp.exp2(m_prev - m_new), jnp.exp2((st_lo - m_new).astype(BF16)),
                jnp.exp2((st_hi - m_new[:, half:]).astype(BF16)))

    def values_diag(vt_ones, e_lo, e_hi):
        pv_lo = jnp.dot(vt_ones[:, :half], e_lo, preferred_element_type=F32)
        pv_hi = jnp.dot(vt_ones[:, half:], e_hi, preferred_element_type=F32)
        return jnp.concatenate([pv_lo[:, :half], pv_lo[:, half:] + pv_hi], axis=1)

    def step(on_diag_strip, use_max):
        items = [(h, s, s == on_diag_strip) for h in range(N_HEADS)
                 for s in range(max(on_diag_strip, 0), n_strips)]

        def scores(u):
            h, s, on_diag = items[u]
            return (scores_diag if on_diag else scores_full)(h, s)

        def softmax(u, *st):
            h, s, on_diag = items[u]
            if not use_max:
                return softmax_bounded(h, s, *st)
            return (softmax_diag if on_diag else softmax_full)(h, s, *st)

        def accumulate(u, alpha, *e):
            h, s, on_diag = items[u]
            rows = slice(h * V_HEAD_DIM, (h + 1) * V_HEAD_DIM)
            cols = strip_cols(s)
            vt_ones = jnp.concatenate([vt_ref[0, rows, :], ones_rows], axis=0)
            pv = (values_diag if on_diag else values_full)(vt_ones, *e)
            if alpha is None:
                acc_ref[rows, cols] += pv[:V_HEAD_DIM]
                l_ref[h, :, cols] += pv[V_HEAD_DIM:V_HEAD_DIM + 1]
            else:
                acc_ref[rows, cols] = alpha * acc_ref[rows, cols] + pv[:V_HEAD_DIM]
                l_ref[h, :, cols] = alpha * l_ref[h, :, cols] + pv[V_HEAD_DIM:V_HEAD_DIM + 1]

        n = len(items)
        st = {0: scores(0), 1: scores(1)}
        pe = {0: softmax(0, *st.pop(0))}
        for u in range(n):
            accumulate(u, *pe.pop(u))
            if u + 2 < n:
                st[u + 2] = scores(u + 2)
            if u + 1 < n:
                pe[u + 1] = softmax(u + 1, *st.pop(u + 1))

    for use_max in (False, True):
        path = bounded if not use_max else jnp.logical_not(bounded)

        @pl.when(jnp.logical_and(diag < 0, path))
        def _():
            step(-1, use_max)

        for on_diag_strip in range(n_strips):
            @pl.when(jnp.logical_and(diag == on_diag_strip, path))
            def _():
                step(on_diag_strip, use_max)

    @pl.when(diag == n_strips - 1)
    def _():
        o = jnp.concatenate(
            [(acc_ref[h * V_HEAD_DIM:(h + 1) * V_HEAD_DIM, :] / l_ref[h]).T.astype(BF16)
             for h in range(N_HEADS)], axis=-1)
        y = jnp.dot(o, wo_ref[...], preferred_element_type=F32)
        o_ref[0] = DEEPNORM_ALPHA * x_ref[0] + gate_ref[0] * y


def _score_bounded(q_norm2, k_norm2, kv_norm_g, v_up_w):
    q_max = jnp.sqrt(jnp.max(q_norm2, axis=-1))
    k_max = lax.cummax(jnp.sqrt(k_norm2[:, :, 0, :N_HEADS]), axis=1)
    scores_ok = jnp.max(q_max * k_max, axis=-1) <= SCORE_BOUND
    v_cap = (KV_LORA_RANK ** 0.5 * jnp.max(jnp.abs(kv_norm_g))
             * jnp.sqrt(jnp.max(jnp.sum(v_up_w * v_up_w, axis=0))) * NORM_SLACK)
    return jnp.logical_and(scores_ok, v_cap <= VALUE_BOUND).astype(jnp.int32)


def _attn_layer(x, qt, k_nope, k_rope, vt, bounded, gate, w_out):
    b, s, d = x.shape
    tq = min(s, ATTN_TILE)
    tk = min(tq, ATTN_KEY_TILE)
    n_strips = tq // tk
    pairs = [(i, j) for i in range(s // tq) for j in range(n_strips * (i + 1))]
    qi_np = np.array([p[0] for p in pairs], np.int32)
    qi = jnp.asarray(qi_np)
    kj = jnp.asarray(np.array([p[1] for p in pairs], np.int32))
    bounded_steps = bounded[:, qi_np].reshape(-1)
    grid_spec = pltpu.PrefetchScalarGridSpec(
        num_scalar_prefetch=3,
        grid=(b, len(pairs)),
        in_specs=[
            pl.BlockSpec((1, Q_ROWS, tq), lambda bi, p, qi, kj, bd: (bi, 0, qi[p])),
            pl.BlockSpec((1, tk, N_HEADS * QK_NOPE_DIM), lambda bi, p, qi, kj, bd: (bi, kj[p], 0)),
            pl.BlockSpec((1, tk, LANES), lambda bi, p, qi, kj, bd: (bi, kj[p], 0)),
            pl.BlockSpec((1, N_HEADS * V_HEAD_DIM, tk), lambda bi, p, qi, kj, bd: (bi, 0, kj[p])),
            pl.BlockSpec((1, tq, d), lambda bi, p, qi, kj, bd: (bi, qi[p], 0)),
            pl.BlockSpec((1, 1, d), lambda bi, p, qi, kj, bd: (bi, 0, 0)),
            _const_spec((d, d)),
        ],
        out_specs=pl.BlockSpec((1, tq, d), lambda bi, p, qi, kj, bd: (bi, qi[p], 0)),
        scratch_shapes=[
            pltpu.VMEM((N_HEADS, 1, tq), F32),
            pltpu.VMEM((N_HEADS, 1, tq), F32),
            pltpu.VMEM((N_HEADS * V_HEAD_DIM, tq), F32),
        ],
    )
    return pl.pallas_call(
        _attn_kernel,
        out_shape=jax.ShapeDtypeStruct((b, s, d), F32),
        grid_spec=grid_spec,
        compiler_params=_params(("arbitrary", "arbitrary")),
        name="attn_layer",
    )(qi, kj, bounded_steps, qt, k_nope, k_rope, vt, x, gate, w_out)


def _swap_halves(w):
    half = w.shape[-1] // 2
    return jnp.concatenate([w[..., half:], w[..., :half]], axis=-1)


def _regroup_q_up(w):
    r = w.shape[0]
    w = w.reshape(r, N_HEADS, QK_NOPE_DIM + QK_ROPE_DIM)
    nope = w[..., :QK_NOPE_DIM].reshape(r, N_HEADS * QK_NOPE_DIM)
    rope = w[..., QK_NOPE_DIM:].reshape(r, N_HEADS * QK_ROPE_DIM)
    return jnp.concatenate([nope, rope], axis=-1).T.astype(BF16)


def kernel(x, c, positions, ada_w, ada_b, ln_g, ln_b, mlp_w1, mlp_w2, pool_w, pool_scale,
           q_down_w, q_norm_g, q_up_w, attn_out_w, kv_in_w, kv_norm_g, k_up_w, v_up_w):
    b, s, d = x.shape
    assert d == D_MODEL and s % min(s, ROW_BLOCK) == 0 and s % min(s, ATTN_TILE) == 0
    mod = _adaln_mod(c, ada_w, ada_b).reshape(DEPTH, b, N_MOD, 1, d)
    tab, tabt = _rope_table(positions)

    kv_w_in = jnp.concatenate([kv_in_w, _swap_halves(kv_in_w[:, KV_LORA_RANK:])],
                              axis=-1).astype(BF16)

    k_nope = k_rope = vt = k_n2 = None
    for l in range(DEPTH):
        shift1, scale1, gate1, shift2, scale2, gate2 = (mod[l, :, m] for m in range(N_MOD))
        g1, b1 = ln_g[l, 0].reshape(1, d), ln_b[l, 0].reshape(1, d)
        g2, b2 = ln_g[l, 1].reshape(1, d), ln_b[l, 1].reshape(1, d)
        if l < N_A_LAYERS:
            xz = x
            pool_args = (shift1, scale1, gate1, pool_w[l].astype(BF16),
                         pool_scale[l].reshape(1, d))
        else:
            if l == N_A_LAYERS:
                k_nope, k_rope, vt, k_n2 = _kv_proj(
                    x, tab, kv_w_in, kv_norm_g.reshape(1, KV_LORA_RANK),
                    k_up_w.astype(BF16), v_up_w.T.astype(BF16))
            jl = l - N_A_LAYERS
            qt, q_n2 = _q_proj(x, tabt, shift1, scale1, q_down_w[jl].astype(BF16),
                               q_norm_g[jl].reshape(1, Q_LORA_RANK), _regroup_q_up(q_up_w[jl]))
            bounded = _score_bounded(q_n2, k_n2, kv_norm_g, v_up_w)
            xz = _attn_layer(x, qt, k_nope, k_rope, vt, bounded, gate1,
                             attn_out_w[jl].astype(BF16))
            pool_args = None
        x = _block_layer(xz, pool_args, g1, b1, shift2, scale2, gate2, mlp_w1[l].astype(BF16),
                         mlp_w2[l].astype(BF16), g2, b2)
    return x
```

```python
import functools

import numpy as np
import jax
import jax.numpy as jnp
from jax import lax
from jax.experimental import pallas as pl
from jax.experimental.pallas import tpu as pltpu

D_MODEL = 1024
DEPTH = 4
N_A_LAYERS = DEPTH // 2
POOL_WINDOWS = (2, 4, 8, 16)
N_POOL_GROUPS = len(POOL_WINDOWS)
POOL_GROUP_DIM = D_MODEL // N_POOL_GROUPS
POOL_HALO = 16
QK_NOPE_DIM = 128
QK_ROPE_DIM = 64
V_HEAD_DIM = 128
N_HEADS = D_MODEL // 128
Q_LORA_RANK = D_MODEL // 2
KV_LORA_RANK = D_MODEL // 4
ROPE_THETA = 10000.0
D_FF = 4 * D_MODEL
N_MOD = 6
DEEPNORM_ALPHA = (2.0 * DEPTH) ** 0.25
LN_EPS = 1e-5
RMS_EPS = 1e-6
ATTN_SCALE = (QK_NOPE_DIM + QK_ROPE_DIM) ** -0.5
Q_SCALE = ATTN_SCALE * float(np.log2(np.e))

LANES = 128
BF16_SUBLANES = 16
ATTN_TILE = 1024
ATTN_KEY_TILE = 512
ROW_BLOCK = 1024
ROPE_COLS = 4096
SUB_ROWS = 256
FF_CHUNK = 1024
ADALN_COLS = 1536
SCORE_BOUND = 60.0
VALUE_BOUND = 2.0 ** 40
NORM_SLACK = 1.0 + 2.0 ** -6
Q_ROWS = N_HEADS * (QK_NOPE_DIM + QK_ROPE_DIM)
ROPE_HALF = QK_ROPE_DIM // 2

VMEM_LIMIT = 56 * 1024 * 1024

F32 = jnp.float32
BF16 = jnp.bfloat16

_NT_DIMS = (((1,), (1,)), ((), ()))


def _params(semantics):
    return pltpu.CompilerParams(dimension_semantics=semantics, vmem_limit_bytes=VMEM_LIMIT)


def _const_spec(shape):
    nd = len(shape)
    return pl.BlockSpec(shape, lambda *_: (0,) * nd, pipeline_mode=pl.Buffered(1))


def _layer_norm(z, g, b):
    mu = jnp.mean(z, axis=-1, keepdims=True)
    zc = z - mu
    var = jnp.mean(zc * zc, axis=-1, keepdims=True)
    return zc * lax.rsqrt(var + LN_EPS) * g + b


def _zero_after(v):
    bits = lax.bitcast_convert_type(v, jnp.uint32)
    bits = lax.shift_right_logical(lax.shift_right_logical(bits, jnp.uint32(16)), jnp.uint32(16))
    return lax.bitcast_convert_type(bits, F32)


def _rope_pair_sum(y):
    return y + pltpu.roll(y, QK_ROPE_DIM, axis=1)


def _adaln_kernel(c_ref, w_ref, b_ref, o_ref):
    c = c_ref[...]
    sc = c * jax.nn.sigmoid(c)
    o_ref[0] = jnp.dot(sc.astype(BF16), w_ref[0].astype(BF16),
                       preferred_element_type=F32) + b_ref[0]


def _adaln_mod(c, ada_w, ada_b):
    b, d = c.shape
    n = ada_w.shape[-1]
    tn = ADALN_COLS
    return pl.pallas_call(
        _adaln_kernel,
        out_shape=jax.ShapeDtypeStruct((DEPTH, b, n), F32),
        grid=(DEPTH, n // tn),
        in_specs=[
            pl.BlockSpec((b, d), lambda l, j: (0, 0)),
            pl.BlockSpec((1, d, tn), lambda l, j: (l, 0, j)),
            pl.BlockSpec((1, 1, tn), lambda l, j: (l, 0, j)),
        ],
        out_specs=pl.BlockSpec((1, b, tn), lambda l, j: (l, 0, j)),
        compiler_params=_params(("arbitrary", "arbitrary")),
        name="adaln_mod",
    )(c, ada_w, ada_b.reshape(DEPTH, 1, n))


def _rope_table_kernel(pos_ref, freq_ref, o_ref, ot_ref):
    pos = pos_ref[0].astype(F32)
    ang = freq_ref[...] * pos
    cos = jnp.cos(ang)
    sin = jnp.sin(ang)
    tab = jnp.concatenate([cos, cos, -sin, sin], axis=0)
    ot_ref[0] = tab
    o_ref[0] = tab.T


def _rope_table(positions):
    b, s = positions.shape
    t = min(s, ROPE_COLS)
    inv_freq = ROPE_THETA ** (-jnp.arange(0, QK_ROPE_DIM, 2, dtype=F32) / QK_ROPE_DIM)
    return pl.pallas_call(
        _rope_table_kernel,
        out_shape=(jax.ShapeDtypeStruct((b, s, LANES), F32),
                   jax.ShapeDtypeStruct((b, LANES, s), F32)),
        grid=(b, s // t),
        in_specs=[
            pl.BlockSpec((1, 1, t), lambda bi, i: (bi, 0, i)),
            pl.BlockSpec((ROPE_HALF, 1), lambda bi, i: (0, 0)),
        ],
        out_specs=(pl.BlockSpec((1, t, LANES), lambda bi, i: (bi, i, 0)),
                   pl.BlockSpec((1, LANES, t), lambda bi, i: (bi, 0, i))),
        compiler_params=_params(("arbitrary", "arbitrary")),
        name="rope_table",
    )(positions.reshape(b, 1, s), inv_freq.reshape(ROPE_HALF, 1))


def _after(v, after):
    return v if after is None else v + _zero_after(after[0:1, :])


def _block_kernel(*refs, pooled, ff_chunk, sub_rows):
    if pooled:
        (x_ref, xh_ref, shift1_ref, scale1_ref, gate1_ref, wp_ref, ps_ref, g1_ref, b1_ref,
         shift_ref, scale_ref, gate_ref, w1_ref, w2_ref, g2_ref, b2_ref,
         o_ref, x1_ref, h_ref, hm_ref) = refs
        in_ref = x_ref
    else:
        (in_ref, g1_ref, b1_ref, shift_ref, scale_ref, gate_ref, w1_ref, w2_ref, g2_ref, b2_ref,
         o_ref, x1_ref, h_ref) = refs
    t = in_ref.shape[1]
    n_sub = t // sub_rows
    n_chunks = D_FF // ff_chunk
    part_rows = sub_rows // n_chunks
    mul = 1.0 + scale_ref[0]
    shift = shift_ref[0]

    def sub(r):
        return slice(r * sub_rows, (r + 1) * sub_rows)

    def normalise(r, lo, hi, after):
        rows = slice(r * sub_rows + lo, r * sub_rows + hi)
        z = x1_ref[rows, :] if pooled else in_ref[0, rows, :]
        x1_ref[rows, :] = _layer_norm(z, _after(g1_ref[...], after), b1_ref[...])

    def modulate(r, lo, hi, after):
        rows = slice(r * sub_rows + lo, r * sub_rows + hi)
        h_ref[rows, :] = (x1_ref[rows, :] * _after(mul, after) + shift).astype(BF16)

    if pooled:
        i = pl.program_id(1)
        mul1 = 1.0 + scale1_ref[0]
        shift1 = shift1_ref[0]
        pending = {}

        def window_mean_minus_token(r, g, after=None):
            w = POOL_WINDOWS[g]
            cols = slice(g * POOL_GROUP_DIM, (g + 1) * POOL_GROUP_DIM)
            level = hm_ref[:, cols]
            if after is not None:
                level = level + _zero_after(after[0:1, cols])
            span = 1
            while span < w:
                level = level + pltpu.roll(level, span, axis=0)
                span *= 2
            row = i * t + r * sub_rows + lax.broadcasted_iota(jnp.int32, (sub_rows, 1), 0)
            cnt = jnp.minimum(row + 1, w).astype(F32)
            return (level[POOL_HALO:] / cnt - hm_ref[POOL_HALO:, cols]).astype(BF16)

        def pool_open(r, after):
            if r == 0:
                hist = jnp.where(i == 0, 0.0, xh_ref[0] * mul1 + shift1)
            else:
                hist = hm_ref[sub_rows:, :]
            hm_ref[:POOL_HALO, :] = hist
            hm_ref[POOL_HALO:, :] = x_ref[0, sub(r), :] * _after(mul1, after) + shift1
            pending[r] = [window_mean_minus_token(r, g) for g in range(2)]

        def pool_wide(r, after):
            pending[r] += [window_mean_minus_token(r, g, after) for g in range(2, N_POOL_GROUPS)]

        def pool_close(r, after):
            ys = pending.pop(r)
            mixed = jnp.concatenate(
                [jnp.dot(y, wp_ref[g], preferred_element_type=F32) for g, y in enumerate(ys)],
                axis=-1) * _after(ps_ref[...], after)
            x1_ref[sub(r), :] = DEEPNORM_ALPHA * x_ref[0, sub(r), :] + gate1_ref[0] * mixed

        def norm_mod(r, after):
            normalise(r, 0, sub_rows, after)
            modulate(r, 0, sub_rows, None)

        stages = [pool_open, pool_wide, pool_close, norm_mod]
        assert len(stages) <= n_chunks

        def enter(r, c, after):
            if c < len(stages):
                stages[c](r, after)
    else:
        def enter(r, c, after):
            normalise(r, c * part_rows, (c + 1) * part_rows, after)
            modulate(r, c * part_rows, (c + 1) * part_rows, after)

    def ff_part(h, c):
        cols = slice(c * ff_chunk, (c + 1) * ff_chunk)
        a = jnp.maximum(jnp.dot(h, w1_ref[:, cols], preferred_element_type=F32), 0.0)
        return jnp.dot((a * a).astype(BF16), w2_ref[cols, :], preferred_element_type=F32)

    def finish(r, y, lo, hi, after=None):
        rows = slice(r * sub_rows + lo, r * sub_rows + hi)
        z2 = DEEPNORM_ALPHA * x1_ref[rows, :] + _after(gate_ref[0], after) * y[lo:hi]
        o_ref[0, rows, :] = _layer_norm(z2, g2_ref[...], b2_ref[...])

    for c in range(n_chunks):
        enter(0, c, None)
    prev = None
    for r in range(n_sub):
        h = h_ref[sub(r), :]
        y = None
        for c in range(n_chunks):
            part = ff_part(h, c)
            y = part if y is None else y + part
            if r + 1 < n_sub:
                enter(r + 1, c, part)
            if prev is not None:
                finish(r - 1, prev, c * part_rows, (c + 1) * part_rows, after=part)
        prev = y
    finish(n_sub - 1, prev, 0, sub_rows)


def _block_layer(xz, pool_args, ln1_g, ln1_b, shift, scale, gate, w1, w2, ln2_g, ln2_b):
    b, s, d = xz.shape
    t = min(s, ROW_BLOCK)
    sub_rows = min(t, SUB_ROWS)
    mod_spec = pl.BlockSpec((1, 1, d), lambda bi, i: (bi, 0, 0))
    row_spec = pl.BlockSpec((1, t, d), lambda bi, i: (bi, i, 0))
    tail_specs = [
        _const_spec((1, d)), _const_spec((1, d)),
        mod_spec, mod_spec, mod_spec,
        _const_spec((d, D_FF)), _const_spec((D_FF, d)),
        _const_spec((1, d)), _const_spec((1, d)),
    ]
    tail_args = (ln1_g, ln1_b, shift, scale, gate, w1, w2, ln2_g, ln2_b)
    scratch = [pltpu.VMEM((t, d), F32), pltpu.VMEM((t, d), BF16)]
    if pool_args is None:
        in_specs, args = [row_spec] + tail_specs, (xz,) + tail_args
    else:
        halo_blocks = t // POOL_HALO
        in_specs = [
            row_spec,
            pl.BlockSpec((1, POOL_HALO, d),
                         lambda bi, i: (bi, jnp.maximum(i * halo_blocks - 1, 0), 0)),
            mod_spec, mod_spec, mod_spec,
            _const_spec((N_POOL_GROUPS, POOL_GROUP_DIM, POOL_GROUP_DIM)),
            _const_spec((1, d)),
        ] + tail_specs
        args = (xz, xz) + tuple(pool_args) + tail_args
        scratch = scratch + [pltpu.VMEM((POOL_HALO + sub_rows, d), F32)]
    return pl.pallas_call(
        functools.partial(_block_kernel, pooled=pool_args is not None, ff_chunk=FF_CHUNK,
                          sub_rows=sub_rows),
        out_shape=jax.ShapeDtypeStruct((b, s, d), F32),
        grid=(b, s // t),
        in_specs=in_specs,
        out_specs=row_spec,
        scratch_shapes=scratch,
        compiler_params=_params(("arbitrary", "arbitrary")),
        name="pool_mlp_layer" if pool_args is not None else "mlp_layer",
    )(*args)


def _kv_kernel(x_ref, tab_ref, win_ref, ng_ref, kup_ref, vup_ref, ind_ref,
               kn_ref, kr_ref, vt_ref, n2_ref):
    xb = x_ref[0].astype(BF16)
    cy = jnp.dot(xb, win_ref[...], preferred_element_type=F32)
    c = cy[:, :KV_LORA_RANK]
    ms = jnp.mean(c * c, axis=-1, keepdims=True)
    ckv = (c * lax.rsqrt(ms + RMS_EPS) * ng_ref[...]).astype(BF16)
    y = cy[:, KV_LORA_RANK:] * tab_ref[0]
    lane = lax.broadcasted_iota(jnp.int32, y.shape, 1)
    kr = jnp.where(lane < QK_ROPE_DIM, _rope_pair_sum(y), 0.0)
    kn = jnp.dot(ckv, kup_ref[...], preferred_element_type=F32)
    kn_ref[0] = kn.astype(BF16)
    kr_ref[0] = kr.astype(BF16)
    nope_n2 = jnp.dot((kn * kn).astype(BF16), ind_ref[...], preferred_element_type=F32)
    key_n2 = (nope_n2 + jnp.sum(kr * kr, axis=1, keepdims=True)) * NORM_SLACK
    n2_ref[0, 0] = jnp.max(key_n2, axis=0, keepdims=True)
    vt_ref[0] = lax.dot_general(vup_ref[...], ckv, _NT_DIMS,
                                preferred_element_type=F32).astype(BF16)


def _kv_proj(x, tab, w_in, norm_g, k_up, v_up_t):
    b, s, d = x.shape
    t = min(s, ATTN_TILE)
    head_of_lane = np.arange(N_HEADS * QK_NOPE_DIM) // QK_NOPE_DIM
    indicator = jnp.asarray(head_of_lane[:, None] == np.arange(LANES)[None, :], BF16)
    return pl.pallas_call(
        _kv_kernel,
        out_shape=(jax.ShapeDtypeStruct((b, s, N_HEADS * QK_NOPE_DIM), BF16),
                   jax.ShapeDtypeStruct((b, s, LANES), BF16),
                   jax.ShapeDtypeStruct((b, N_HEADS * V_HEAD_DIM, s), BF16),
                   jax.ShapeDtypeStruct((b, s // t, 1, LANES), F32)),
        grid=(b, s // t),
        in_specs=[
            pl.BlockSpec((1, t, d), lambda bi, i: (bi, i, 0)),
            pl.BlockSpec((1, t, LANES), lambda bi, i: (bi, i, 0)),
            _const_spec((d, KV_LORA_RANK + LANES)),
            _const_spec((1, KV_LORA_RANK)),
            _const_spec((KV_LORA_RANK, N_HEADS * QK_NOPE_DIM)),
            _const_spec((N_HEADS * V_HEAD_DIM, KV_LORA_RANK)),
            _const_spec((N_HEADS * QK_NOPE_DIM, LANES)),
        ],
        out_specs=(pl.BlockSpec((1, t, N_HEADS * QK_NOPE_DIM), lambda bi, i: (bi, i, 0)),
                   pl.BlockSpec((1, t, LANES), lambda bi, i: (bi, i, 0)),
                   pl.BlockSpec((1, N_HEADS * V_HEAD_DIM, t), lambda bi, i: (bi, 0, i)),
                   pl.BlockSpec((1, 1, 1, LANES), lambda bi, i: (bi, i, 0, 0))),
        compiler_params=_params(("arbitrary", "arbitrary")),
        name="kv_proj",
    )(x, tab, w_in, norm_g, k_up, v_up_t, indicator)


def _q_kernel(x_ref, tabt_ref, shift_ref, scale_ref, wd_ref, ng_ref, wq_ref, qt_ref, n2_ref):
    h = (x_ref[0] * (1.0 + scale_ref[0]) + shift_ref[0]).astype(BF16)
    c = jnp.dot(h, wd_ref[...], preferred_element_type=F32)
    ms = jnp.mean(c * c, axis=-1, keepdims=True)
    cq = (c * lax.rsqrt(ms + RMS_EPS) * ng_ref[...]).astype(BF16)
    qa = lax.dot_general(wq_ref[...], cq, _NT_DIMS, preferred_element_type=F32)
    tabt = tabt_ref[0]
    cos2, sin2 = tabt[:QK_ROPE_DIM], tabt[QK_ROPE_DIM:]
    t = tabt.shape[1]
    rope0 = N_HEADS * QK_NOPE_DIM
    n2 = []
    for hd in range(N_HEADS):
        nope = qa[hd * LANES:(hd + 1) * LANES] * Q_SCALE
        qt_ref[0, hd * LANES:(hd + 1) * LANES, :] = nope.astype(BF16)
        rope_rows = slice(rope0 + hd * QK_ROPE_DIM, rope0 + (hd + 1) * QK_ROPE_DIM)
        a = qa[rope_rows]
        swapped = jnp.concatenate([a[ROPE_HALF:], a[:ROPE_HALF]], axis=0)
        rot = (a * cos2 + swapped * sin2) * Q_SCALE
        qt_ref[0, rope_rows, :] = rot.astype(BF16)
        query_n2 = (jnp.sum(nope * nope, axis=0, keepdims=True)
                    + jnp.sum(rot * rot, axis=0, keepdims=True))
        folded = query_n2[:, :LANES]
        for c in range(1, t // LANES):
            folded = jnp.maximum(folded, query_n2[:, c * LANES:(c + 1) * LANES])
        n2.append(folded)
    n2_ref[0, 0] = jnp.concatenate(n2, axis=0)


def _q_proj(x, tabt, shift, scale, w_down, norm_g, w_q_t):
    b, s, d = x.shape
    t = min(s, ATTN_TILE)
    mod_spec = pl.BlockSpec((1, 1, d), lambda bi, i: (bi, 0, 0))
    return pl.pallas_call(
        _q_kernel,
        out_shape=(jax.ShapeDtypeStruct((b, Q_ROWS, s), BF16),
                   jax.ShapeDtypeStruct((b, s // t, N_HEADS, LANES), F32)),
        grid=(b, s // t),
        in_specs=[
            pl.BlockSpec((1, t, d), lambda bi, i: (bi, i, 0)),
            pl.BlockSpec((1, LANES, t), lambda bi, i: (bi, 0, i)),
            mod_spec, mod_spec,
            _const_spec((d, Q_LORA_RANK)), _const_spec((1, Q_LORA_RANK)),
            _const_spec((Q_ROWS, Q_LORA_RANK)),
        ],
        out_specs=(pl.BlockSpec((1, Q_ROWS, t), lambda bi, i: (bi, 0, i)),
                   pl.BlockSpec((1, 1, N_HEADS, LANES), lambda bi, i: (bi, i, 0, 0))),
        compiler_params=_params(("arbitrary", "arbitrary")),
        name="q_proj",
    )(x, tabt, shift, scale, w_down, norm_g, w_q_t)


def _attn_kernel(qi_ref, kj_ref, bounded_ref, qt_ref, kn_ref, kr_ref, vt_ref, x_ref, gate_ref,
                 wo_ref, o_ref, m_ref, l_ref, acc_ref):
    p = pl.program_id(1)
    i = qi_ref[p]
    j = kj_ref[p]
    bounded = bounded_ref[pl.program_id(0) * pl.num_programs(1) + p] != 0
    tk = kn_ref.shape[1]
    n_strips = qt_ref.shape[2] // tk
    diag = j - n_strips * i

    @pl.when(j == 0)
    def _():
        m_ref[...] = jnp.full(m_ref.shape, -jnp.inf, F32)
        l_ref[...] = jnp.zeros(l_ref.shape, F32)
        acc_ref[...] = jnp.zeros(acc_ref.shape, F32)

    half = tk // 2
    ones_rows = jnp.ones((BF16_SUBLANES, tk), BF16)

    def strip_cols(s):
        return slice(s * tk, (s + 1) * tk)

    def operands(h, s):
        k = jnp.concatenate([kn_ref[0, :, h * QK_NOPE_DIM:(h + 1) * QK_NOPE_DIM], kr_ref[0]],
                            axis=1)
        rope0 = N_HEADS * QK_NOPE_DIM + h * QK_ROPE_DIM
        q_rope = qt_ref[0, rope0:rope0 + QK_ROPE_DIM, strip_cols(s)]
        qt = jnp.concatenate(
            [qt_ref[0, h * QK_NOPE_DIM:(h + 1) * QK_NOPE_DIM, strip_cols(s)], q_rope, q_rope],
            axis=0)
        return k, qt

    def scores_full(h, s):
        k, qt = operands(h, s)
        return (jnp.dot(k, qt, preferred_element_type=F32),)

    def softmax_full(h, s, st):
        m_prev = m_ref[h, :, strip_cols(s)]
        m_new = jnp.maximum(m_prev, jnp.max(st, axis=0, keepdims=True))
        m_ref[h, :, strip_cols(s)] = m_new
        return jnp.exp2(m_prev - m_new), jnp.exp2((st - m_new).astype(BF16))

    def softmax_bounded(h, s, *sts):
        return (None,) + tuple(jnp.exp2(st).astype(BF16) for st in sts)

    def values_full(vt_ones, e):
        return jnp.dot(vt_ones, e, preferred_element_type=F32)

    def scores_diag(h, s):
        k, qt = operands(h, s)
        tri = (lax.broadcasted_iota(jnp.int32, (half, half), 0)
               <= lax.broadcasted_iota(jnp.int32, (half, half), 1))
        st_lo = jnp.dot(k[:half], qt, preferred_element_type=F32)
        st_hi = jnp.dot(k[half:], qt[:, half:], preferred_element_type=F32)
        st_lo = jnp.concatenate([jnp.where(tri, st_lo[:, :half], -jnp.inf), st_lo[:, half:]],
                                axis=1)
        return st_lo, jnp.where(tri, st_hi, -jnp.inf)

    def softmax_diag(h, s, st_lo, st_hi):
        m_prev = m_ref[h, :, strip_cols(s)]
        mx_lo = jnp.max(st_lo, axis=0, keepdims=True)
        mx_hi = jnp.max(st_hi, axis=0, keepdims=True)
        m_blk = jnp.concatenate([mx_lo[:, :half], jnp.maximum(mx_lo[:, half:], mx_hi)], axis=1)
        m_new = jnp.maximum(m_prev, m_blk)
        m_ref[h, :, strip_cols(s)] = m_new
        return (jnp.exp2(m_prev - m_new), jnp.exp2((st_lo - m_new).astype(BF16)),
                jnp.exp2((st_hi - m_new[:, half:]).astype(BF16)))

    def values_diag(vt_ones, e_lo, e_hi):
        pv_lo = jnp.dot(vt_ones[:, :half], e_lo, preferred_element_type=F32)
        pv_hi = jnp.dot(vt_ones[:, half:], e_hi, preferred_element_type=F32)
        return jnp.concatenate([pv_lo[:, :half], pv_lo[:, half:] + pv_hi], axis=1)

    def step(on_diag_strip, use_max):
        items = [(h, s, s == on_diag_strip) for h in range(N_HEADS)
                 for s in range(max(on_diag_strip, 0), n_strips)]

        def scores(u):
            h, s, on_diag = items[u]
            return (scores_diag if on_diag else scores_full)(h, s)

        def softmax(u, *st):
            h, s, on_diag = items[u]
            if not use_max:
                return softmax_bounded(h, s, *st)
            return (softmax_diag if on_diag else softmax_full)(h, s, *st)

        def accumulate(u, alpha, *e):
            h, s, on_diag = items[u]
            rows = slice(h * V_HEAD_DIM, (h + 1) * V_HEAD_DIM)
            cols = strip_cols(s)
            vt_ones = jnp.concatenate([vt_ref[0, rows, :], ones_rows], axis=0)
            pv = (values_diag if on_diag else values_full)(vt_ones, *e)
            if alpha is None:
                acc_ref[rows, cols] += pv[:V_HEAD_DIM]
                l_ref[h, :, cols] += pv[V_HEAD_DIM:V_HEAD_DIM + 1]
            else:
                acc_ref[rows, cols] = alpha * acc_ref[rows, cols] + pv[:V_HEAD_DIM]
                l_ref[h, :, cols] = alpha * l_ref[h, :, cols] + pv[V_HEAD_DIM:V_HEAD_DIM + 1]

        n = len(items)
        st = {0: scores(0), 1: scores(1)}
        pe = {0: softmax(0, *st.pop(0))}
        for u in range(n):
            accumulate(u, *pe.pop(u))
            if u + 2 < n:
                st[u + 2] = scores(u + 2)
            if u + 1 < n:
                pe[u + 1] = softmax(u + 1, *st.pop(u + 1))

    for use_max in (False, True):
        path = bounded if not use_max else jnp.logical_not(bounded)

        @pl.when(jnp.logical_and(diag < 0, path))
        def _():
            step(-1, use_max)

        for on_diag_strip in range(n_strips):
            @pl.when(jnp.logical_and(diag == on_diag_strip, path))
            def _():
                step(on_diag_strip, use_max)

    @pl.when(diag == n_strips - 1)
    def _():
        o = jnp.concatenate(
            [(acc_ref[h * V_HEAD_DIM:(h + 1) * V_HEAD_DIM, :] / l_ref[h]).T.astype(BF16)
             for h in range(N_HEADS)], axis=-1)
        y = jnp.dot(o, wo_ref[...], preferred_element_type=F32)
        o_ref[0] = DEEPNORM_ALPHA * x_ref[0] + gate_ref[0] * y


def _score_bounded(q_norm2, k_norm2, kv_norm_g, v_up_w):
    q_max = jnp.sqrt(jnp.max(q_norm2, axis=-1))
    k_max = lax.cummax(jnp.sqrt(k_norm2[:, :, 0, :N_HEADS]), axis=1)
    scores_ok = jnp.max(q_max * k_max, axis=-1) <= SCORE_BOUND
    v_cap = (KV_LORA_RANK ** 0.5 * jnp.max(jnp.abs(kv_norm_g))
             * jnp.sqrt(jnp.max(jnp.sum(v_up_w * v_up_w, axis=0))) * NORM_SLACK)
    return jnp.logical_and(scores_ok, v_cap <= VALUE_BOUND).astype(jnp.int32)


def _attn_layer(x, qt, k_nope, k_rope, vt, bounded, gate, w_out):
    b, s, d = x.shape
    tq = min(s, ATTN_TILE)
    tk = min(tq, ATTN_KEY_TILE)
    n_strips = tq // tk
    pairs = [(i, j) for i in range(s // tq) for j in range(n_strips * (i + 1))]
    qi_np = np.array([p[0] for p in pairs], np.int32)
    qi = jnp.asarray(qi_np)
    kj = jnp.asarray(np.array([p[1] for p in pairs], np.int32))
    bounded_steps = bounded[:, qi_np].reshape(-1)
    grid_spec = pltpu.PrefetchScalarGridSpec(
        num_scalar_prefetch=3,
        grid=(b, len(pairs)),
        in_specs=[
            pl.BlockSpec((1, Q_ROWS, tq), lambda bi, p, qi, kj, bd: (bi, 0, qi[p])),
            pl.BlockSpec((1, tk, N_HEADS * QK_NOPE_DIM), lambda bi, p, qi, kj, bd: (bi, kj[p], 0)),
            pl.BlockSpec((1, tk, LANES), lambda bi, p, qi, kj, bd: (bi, kj[p], 0)),
            pl.BlockSpec((1, N_HEADS * V_HEAD_DIM, tk), lambda bi, p, qi, kj, bd: (bi, 0, kj[p])),
            pl.BlockSpec((1, tq, d), lambda bi, p, qi, kj, bd: (bi, qi[p], 0)),
            pl.BlockSpec((1, 1, d), lambda bi, p, qi, kj, bd: (bi, 0, 0)),
            _const_spec((d, d)),
        ],
        out_specs=pl.BlockSpec((1, tq, d), lambda bi, p, qi, kj, bd: (bi, qi[p], 0)),
        scratch_shapes=[
            pltpu.VMEM((N_HEADS, 1, tq), F32),
            pltpu.VMEM((N_HEADS, 1, tq), F32),
            pltpu.VMEM((N_HEADS * V_HEAD_DIM, tq), F32),
        ],
    )
    return pl.pallas_call(
        _attn_kernel,
        out_shape=jax.ShapeDtypeStruct((b, s, d), F32),
        grid_spec=grid_spec,
        compiler_params=_params(("arbitrary", "arbitrary")),
        name="attn_layer",
    )(qi, kj, bounded_steps, qt, k_nope, k_rope, vt, x, gate, w_out)


def _swap_halves(w):
    half = w.shape[-1] // 2
    return jnp.concatenate([w[..., half:], w[..., :half]], axis=-1)


def _regroup_q_up(w):
    r = w.shape[0]
    w = w.reshape(r, N_HEADS, QK_NOPE_DIM + QK_ROPE_DIM)
    nope = w[..., :QK_NOPE_DIM].reshape(r, N_HEADS * QK_NOPE_DIM)
    rope = w[..., QK_NOPE_DIM:].reshape(r, N_HEADS * QK_ROPE_DIM)
    return jnp.concatenate([nope, rope], axis=-1).T.astype(BF16)


def kernel(x, c, positions, ada_w, ada_b, ln_g, ln_b, mlp_w1, mlp_w2, pool_w, pool_scale,
           q_down_w, q_norm_g, q_up_w, attn_out_w, kv_in_w, kv_norm_g, k_up_w, v_up_w):
    b, s, d = x.shape
    assert d == D_MODEL and s % min(s, ROW_BLOCK) == 0 and s % min(s, ATTN_TILE) == 0
    mod = _adaln_mod(c, ada_w, ada_b).reshape(DEPTH, b, N_MOD, 1, d)
    tab, tabt = _rope_table(positions)

    kv_w_in = jnp.concatenate([kv_in_w, _swap_halves(kv_in_w[:, KV_LORA_RANK:])],
                              axis=-1).astype(BF16)

    k_nope = k_rope = vt = k_n2 = None
    for l in range(DEPTH):
        shift1, scale1, gate1, shift2, scale2, gate2 = (mod[l, :, m] for m in range(N_MOD))
        g1, b1 = ln_g[l, 0].reshape(1, d), ln_b[l, 0].reshape(1, d)
        g2, b2 = ln_g[l, 1].reshape(1, d), ln_b[l, 1].reshape(1, d)
        if l < N_A_LAYERS:
            xz = x
            pool_args = (shift1, scale1, gate1, pool_w[l].astype(BF16),
                         pool_scale[l].reshape(1, d))
        else:
            if l == N_A_LAYERS:
                k_nope, k_rope, vt, k_n2 = _kv_proj(
                    x, tab, kv_w_in, kv_norm_g.reshape(1, KV_LORA_RANK),
                    k_up_w.astype(BF16), v_up_w.T.astype(BF16))
            jl = l - N_A_LAYERS
            qt, q_n2 = _q_proj(x, tabt, shift1, scale1, q_down_w[jl].astype(BF16),
                               q_norm_g[jl].reshape(1, Q_LORA_RANK), _regroup_q_up(q_up_w[jl]))
            bounded = _score_bounded(q_n2, k_n2, kv_norm_g, v_up_w)
            xz = _attn_layer(x, qt, k_nope, k_rope, vt, bounded, gate1,
                             attn_out_w[jl].astype(BF16))
            pool_args = None
        x = _block_layer(xz, pool_args, g1, b1, shift2, scale2, gate2, mlp_w1[l].astype(BF16),
                         mlp_w2[l].astype(BF16), g2, b2)
    return x
```
